```python
import jax
import jax.numpy as jnp
from jax import lax
import numpy as np

D_MODEL = 1024
BATCH = 16
SEQ = 2048
DEPTH = 2

CTX_LEN = 256
GRID_W = 64
N_MOD = 9
D_FF = ((8 * D_MODEL // 3 + 127) // 128) * 128
MIX_WIDTH = D_MODEL
POOL_WIDTH = D_MODEL // 4
POOL_WINDOWS = (2, 4, 8, 16)
POOL_GROUPS = len(POOL_WINDOWS)
POOL_GROUP = POOL_WIDTH // POOL_GROUPS
RET_WIDTH = D_MODEL // 2
RET_HEADS = 4
RET_HEAD_DIM = RET_WIDTH // RET_HEADS
RET_CHUNK = 128
CONV_WIDTH = D_MODEL // 4
CONV_K = 31
IN_WIDTH = POOL_WIDTH + 4 * RET_WIDTH + 2 * CONV_WIDTH
Q_OFF = POOL_WIDTH
K_OFF = Q_OFF + RET_WIDTH
V_OFF = K_OFF + RET_WIDTH
G_OFF = V_OFF + RET_WIDTH
C_OFF = G_OFF + RET_WIDTH
SPLITS = (Q_OFF, K_OFF, V_OFF, G_OFF, C_OFF)
ROPE_BASE = 10000.0
EPS = 1e-6
F32 = jnp.float32

kernel_name = 'hybrid_pool_retention_conv_dit'


def _rmsnorm(x, g):
    x32 = x.astype(F32)
    y = x32 * lax.rsqrt(jnp.mean(x32 * x32, axis=-1, keepdims=True) + EPS)
    return (y * g.astype(F32)).astype(x.dtype)


def _modulate(h, shift, scale):
    return h * (1 + scale) + shift


def _swiglu(h, w1, w3, w2):
    return (jax.nn.silu(h @ w1) * (h @ w3)) @ w2


def _multiscale_pool(p, pool_w, pool_scale):
    B, L, _ = p.shape
    p32 = p.astype(F32)
    cs = jnp.concatenate([jnp.zeros((B, 1, POOL_WIDTH), F32), jnp.cumsum(p32, axis=1)], axis=1)
    t = jnp.arange(L)
    outs = []
    for gi, w in enumerate(POOL_WINDOWS):
        lo = jnp.clip(t - w // 2, 0, L)
        hi = jnp.clip(t + w // 2, 0, L)
        csg = cs[:, :, gi * POOL_GROUP:(gi + 1) * POOL_GROUP]
        cnt = (hi - lo).astype(F32)[None, :, None]
        outs.append((csg[:, hi] - csg[:, lo]) / cnt - p32[:, :, gi * POOL_GROUP:(gi + 1) * POOL_GROUP])
    pooled = jnp.stack(outs, axis=2)
    mixed = jnp.einsum('blgc,gcd->blgd', pooled, pool_w.astype(F32)).reshape(B, L, POOL_WIDTH)
    return (mixed * pool_scale.astype(F32)).astype(p.dtype)


def _conv_module(u, dw, db, ln_g, ln_b):
    a, gt = jnp.split(u, 2, axis=-1)
    z = a * jax.nn.sigmoid(gt)
    z = lax.conv_general_dilated(z, dw[:, None, :].astype(z.dtype), window_strides=(1,),
                                 padding=[(CONV_K // 2, CONV_K // 2)],
                                 dimension_numbers=('NWC', 'WIO', 'NWC'),
                                 feature_group_count=CONV_WIDTH) + db
    z32 = z.astype(F32)
    mu = jnp.mean(z32, axis=-1, keepdims=True)
    var = jnp.mean(jnp.square(z32 - mu), axis=-1, keepdims=True)
    z32 = (z32 - mu) * lax.rsqrt(var + EPS) * ln_g.astype(F32) + ln_b.astype(F32)
    return jax.nn.silu(z32).astype(u.dtype)


def _heads(t):
    return t.reshape(t.shape[0], t.shape[1], RET_HEADS, RET_HEAD_DIM)


def _rotate(u, ang):
    u1, u2 = jnp.split(u, 2, axis=-1)
    cos = jnp.cos(ang)[None, :, None, :]
    sin = jnp.sin(ang)[None, :, None, :]
    return jnp.concatenate([u1 * cos - u2 * sin, u1 * sin + u2 * cos], axis=-1)


def _axial_rope(t, row, col):
    n_freq = RET_HEAD_DIM // 4
    inv = ROPE_BASE ** (-jnp.arange(n_freq, dtype=F32) / n_freq)
    tr, tc = jnp.split(t.astype(F32), 2, axis=-1)
    return jnp.concatenate([_rotate(tr, row[:, None] * inv[None]),
                            _rotate(tc, col[:, None] * inv[None])], axis=-1)


def _retention_dir(q, k, v, log_gamma, s0, strict):
    B, L, H, Dk = q.shape
    Dv = v.shape[-1]
    n = L // RET_CHUNK
    qc = q.astype(F32).reshape(B, n, RET_CHUNK, H, Dk)
    kc = k.astype(F32).reshape(B, n, RET_CHUNK, H, Dk)
    vc = v.astype(F32).reshape(B, n, RET_CHUNK, H, Dv)
    pos = jnp.arange(RET_CHUNK, dtype=F32)
    diff = pos[:, None] - pos[None, :]
    mask = (diff > 0) if strict else (diff >= 0)
    decay = jnp.where(mask[None], jnp.exp(jnp.maximum(diff, 0.0)[None] * log_gamma[:, None, None]), 0.0)
    scores = jnp.einsum('bnihd,bnjhd->bnhij', qc, kc) * decay[None, None]
    intra = jnp.einsum('bnhij,bnjhe->bnihe', scores, vc)
    w_k = jnp.exp((RET_CHUNK - 1 - pos)[:, None] * log_gamma[None, :])
    kv = jnp.einsum('bnjhd,jh,bnjhe->bnhde', kc, w_k, vc)
    chunk_decay = jnp.exp(RET_CHUNK * log_gamma)[None, :, None, None]

    def step(s, kv_n):
        return chunk_decay * s + kv_n, s

    s_final, s_starts = lax.scan(step, s0, jnp.moveaxis(kv, 1, 0))
    s_starts = jnp.moveaxis(s_starts, 0, 1)
    w_q = jnp.exp((pos + 1.0)[:, None] * log_gamma[None, :])
    cross = jnp.einsum('bnihd,ih,bnhde->bnihe', qc, w_q, s_starts)
    return (intra + cross).reshape(B, L, H, Dv), s_final


def _context_states(k, v, log_gamma_f, log_gamma_b):
    Lc = k.shape[1]
    pos = jnp.arange(Lc, dtype=F32)
    w_f = jnp.exp((Lc - 1 - pos)[:, None] * log_gamma_f[None, :])
    w_b = jnp.exp(pos[:, None] * log_gamma_b[None, :])
    k32 = k.astype(F32)
    v32 = v.astype(F32)
    s_f = jnp.einsum('bjhd,jh,bjhe->bhde', k32, w_f, v32)
    s_b = jnp.einsum('bjhd,jh,bjhe->bhde', k32, w_b, v32)
    return s_f, s_b


def _retention_readout(o, g, gn_g):
    B, L, H, Dv = o.shape
    mu = jnp.mean(o, axis=-1, keepdims=True)
    var = jnp.mean(jnp.square(o - mu), axis=-1, keepdims=True)
    y = ((o - mu) * lax.rsqrt(var + EPS)).reshape(B, L, H * Dv) * gn_g.astype(F32)
    return (y * jax.nn.silu(g.astype(F32))).astype(g.dtype)


def _mixer(hx, hy, w_in, w_out, pool_w, pool_scale, dec_f, dec_b, gn_g,
           conv_dw, conv_b, conv_ln_g, conv_ln_b, need_ctx_out):
    B, L, _ = hx.shape
    lg_f = jax.nn.log_sigmoid(dec_f.astype(F32))
    lg_b = jax.nn.log_sigmoid(dec_b.astype(F32))
    k_scale = RET_HEAD_DIM ** -0.5
    rows = L // GRID_W
    row = jnp.repeat(jnp.arange(rows, dtype=F32), GRID_W)
    col = jnp.tile(jnp.arange(GRID_W, dtype=F32), rows)

    if need_ctx_out:
        pool_y, qy, ky, vy, gy, conv_y = jnp.split(hy @ w_in, SPLITS, axis=-1)
        qy = _heads(qy)
        ky = _heads(ky) * k_scale
        vy = _heads(vy)
        s0 = jnp.zeros((hy.shape[0], RET_HEADS, RET_HEAD_DIM, RET_HEAD_DIM), F32)
        oy_f, s_f = _retention_dir(qy, ky, vy, lg_f, s0, False)
        oy_b, s_b = _retention_dir(jnp.flip(qy, 1), jnp.flip(ky, 1), jnp.flip(vy, 1), lg_b, s0, True)
        ret_y = _retention_readout(oy_f + jnp.flip(oy_b, 1), gy, gn_g)
        cat_y = jnp.concatenate([_multiscale_pool(pool_y, pool_w, pool_scale), ret_y,
                                 _conv_module(conv_y, conv_dw, conv_b, conv_ln_g, conv_ln_b)], axis=-1)
        out_y = cat_y @ w_out
    else:
        ky, vy = jnp.split(hy @ w_in[:, K_OFF:G_OFF], 2, axis=-1)
        s_f, s_b = _context_states(_heads(ky) * k_scale, _heads(vy), lg_f, lg_b)
        out_y = None

    pool_x, qx, kx, vx, gx, conv_x = jnp.split(hx @ w_in, SPLITS, axis=-1)
    qx = _axial_rope(_heads(qx), row, col)
    kx = _axial_rope(_heads(kx), row, col) * k_scale
    vx = _heads(vx)
    ox_f, _ = _retention_dir(qx, kx, vx, lg_f, s_f, False)
    ox_b, _ = _retention_dir(jnp.flip(qx, 1), jnp.flip(kx, 1), jnp.flip(vx, 1), lg_b, s_b, True)
    ret_x = _retention_readout(ox_f + jnp.flip(ox_b, 1), gx, gn_g)
    cat_x = jnp.concatenate([_multiscale_pool(pool_x, pool_w, pool_scale), ret_x,
                             _conv_module(conv_x, conv_dw, conv_b, conv_ln_g, conv_ln_b)], axis=-1)
    return cat_x @ w_out, out_y


def setup_inputs(seed: int = 0) -> dict:
    key = jax.random.key(seed)
    ks = jax.random.split(key, 24)

    def nrm(k, shape, scale):
        return jax.random.normal(k, shape, F32) * scale

    decay_base = jnp.log(2.0 ** (5.0 + jnp.arange(RET_HEADS, dtype=F32)) - 1.0)
    return {
        'x': nrm(ks[0], (BATCH, SEQ, D_MODEL), 1.0),
        'c': nrm(ks[1], (BATCH, D_MODEL), 1.0),
        'ctx': nrm(ks[2], (BATCH, CTX_LEN, D_MODEL), 1.0),
        'c_ctx': nrm(ks[3], (D_MODEL,), 1.0),
        'w_mod': nrm(ks[4], (DEPTH, D_MODEL, N_MOD * D_MODEL), 0.02),
        'b_mod': nrm(ks[5], (DEPTH, N_MOD * D_MODEL), 0.02),
        'norm_g': 1.0 + nrm(ks[6], (DEPTH, 3, D_MODEL), 0.02),
        'ffn_w1': nrm(ks[7], (DEPTH, 2, D_MODEL, D_FF), D_MODEL ** -0.5),
        'ffn_w3': nrm(ks[8], (DEPTH, 2, D_MODEL, D_FF), D_MODEL ** -0.5),
        'ffn_w2': nrm(ks[9], (DEPTH, 2, D_FF, D_MODEL), D_FF ** -0.5),
        'w_in': nrm(ks[10], (DEPTH, D_MODEL, IN_WIDTH), D_MODEL ** -0.5),
        'w_out': nrm(ks[11], (DEPTH, MIX_WIDTH, D_MODEL), MIX_WIDTH ** -0.5),
        'pool_w': nrm(ks[12], (DEPTH, POOL_GROUPS, POOL_GROUP, POOL_GROUP), POOL_GROUP ** -0.5),
        'pool_scale': 1.0 + nrm(ks[13], (DEPTH, POOL_WIDTH), 0.02),
        'ret_decay_fwd': decay_base + nrm(ks[14], (DEPTH, RET_HEADS), 0.1),
        'ret_decay_bwd': decay_base + nrm(ks[15], (DEPTH, RET_HEADS), 0.1),
        'ret_gn_g': 1.0 + nrm(ks[16], (DEPTH, RET_WIDTH), 0.02),
        'conv_dw': nrm(ks[17], (DEPTH, CONV_K, CONV_WIDTH), CONV_K ** -0.5),
        'conv_b': nrm(ks[18], (DEPTH, CONV_WIDTH), 0.02),
        'conv_ln_g': 1.0 + nrm(ks[19], (DEPTH, CONV_WIDTH), 0.02),
        'conv_ln_b': nrm(ks[20], (DEPTH, CONV_WIDTH), 0.02),
        'final_g': 1.0 + nrm(ks[21], (D_MODEL,), 0.02),
    }


def reference(x, c, ctx, c_ctx, w_mod, b_mod, norm_g, ffn_w1, ffn_w3, ffn_w2, w_in, w_out,
              pool_w, pool_scale, ret_decay_fwd, ret_decay_bwd, ret_gn_g, conv_dw, conv_b,
              conv_ln_g, conv_ln_b, final_g):
    y = ctx
    for l in range(DEPTH):
        last = l == DEPTH - 1
        mx = jnp.split((jax.nn.silu(c) @ w_mod[l] + b_mod[l])[:, None, :], N_MOD, axis=-1)
        my = jnp.split((jax.nn.silu(c_ctx) @ w_mod[l] + b_mod[l])[None, None, :], N_MOD, axis=-1)

        x = x + 0.5 * mx[2] * _swiglu(_modulate(_rmsnorm(x, norm_g[l, 0]), mx[0], mx[1]),
                                      ffn_w1[l, 0], ffn_w3[l, 0], ffn_w2[l, 0])
        y = y + 0.5 * my[2] * _swiglu(_modulate(_rmsnorm(y, norm_g[l, 0]), my[0], my[1]),
                                      ffn_w1[l, 0], ffn_w3[l, 0], ffn_w2[l, 0])

        hx = _modulate(_rmsnorm(x, norm_g[l, 1]), mx[3], mx[4])
        hy = _modulate(_rmsnorm(y, norm_g[l, 1]), my[3], my[4])
        ox, oy = _mixer(hx, hy, w_in[l], w_out[l], pool_w[l], pool_scale[l], ret_decay_fwd[l],
                        ret_decay_bwd[l], ret_gn_g[l], conv_dw[l], conv_b[l], conv_ln_g[l],
                        conv_ln_b[l], not last)
        x = x + mx[5] * ox
        if not last:
            y = y + my[5] * oy

        x = x + 0.5 * mx[8] * _swiglu(_modulate(_rmsnorm(x, norm_g[l, 2]), mx[6], mx[7]),
                                      ffn_w1[l, 1], ffn_w3[l, 1], ffn_w2[l, 1])
        if not last:
            y = y + 0.5 * my[8] * _swiglu(_modulate(_rmsnorm(y, norm_g[l, 2]), my[6], my[7]),
                                          ffn_w1[l, 1], ffn_w3[l, 1], ffn_w2[l, 1])
    return _rmsnorm(x, final_g)
```

```python
import functools

import jax
import jax.numpy as jnp
from jax import lax
from jax.experimental import pallas as pl
from jax.experimental.pallas import tpu as pltpu

F32 = jnp.float32
BF16 = jnp.bfloat16

D_MODEL = 1024
N_MOD = 9
D_FF = 2816
POOL_WIDTH = 256
POOL_WINDOWS = (2, 4, 8, 16)
POOL_GROUP = 64
RET_WIDTH = 512
RET_HEADS = 4
HEAD_DIM = 128
CHUNK = 128
CONV_WIDTH = 256
CONV_K = 31
IN_WIDTH = 2816
Q_OFF = POOL_WIDTH
K_OFF = Q_OFF + RET_WIDTH
V_OFF = K_OFF + RET_WIDTH
G_OFF = V_OFF + RET_WIDTH
C_OFF = G_OFF + RET_WIDTH
GRID_W = 64
ROPE_BASE = 10000.0
EPS = 1e-6

HALO = 16
PZ_WIDTH = POOL_WIDTH + CONV_WIDTH
MOD_ROWS = 24
VMEM_LIMIT = 56 * 1024 * 1024


def _cparams(n_axes):
    return pltpu.CompilerParams(dimension_semantics=("arbitrary",) * n_axes,
                                vmem_limit_bytes=VMEM_LIMIT)


def _const_spec(shape):
    nd = len(shape)
    return pl.BlockSpec(shape, lambda *_: (0,) * nd, pipeline_mode=pl.Buffered(1))


def _sigmoid(x):
    return 1.0 / (1.0 + jnp.exp(-x))


def _silu(x):
    return x * _sigmoid(x)


def _rms_mod(x, g, shift, scale):
    y = x * lax.rsqrt(jnp.mean(x * x, axis=-1, keepdims=True) + EPS) * g
    return y * (1.0 + scale) + shift


def _mod_kernel(c_ref, w_ref, b_ref, o_ref):
    s = _silu(c_ref[...]).astype(BF16)
    o_ref[0] = jnp.dot(s, w_ref[0].astype(BF16), preferred_element_type=F32) + b_ref[0]


def _modulation(cvec, w_mod, b_mod):
    depth = w_mod.shape[0]
    tn = 1152
    return pl.pallas_call(
        _mod_kernel,
        grid=(depth, N_MOD * D_MODEL // tn),
        in_specs=[pl.BlockSpec((MOD_ROWS, D_MODEL), lambda l, j: (0, 0)),
                  pl.BlockSpec((1, D_MODEL, tn), lambda l, j: (l, 0, j)),
                  pl.BlockSpec((1, 1, tn), lambda l, j: (l, 0, j))],
        out_specs=pl.BlockSpec((1, MOD_ROWS, tn), lambda l, j: (l, 0, j)),
        out_shape=jax.ShapeDtypeStruct((depth, MOD_ROWS, N_MOD * D_MODEL), F32),
        compiler_params=_cparams(2),
        name="modulation",
    )(cvec, w_mod, b_mod.reshape(depth, 1, N_MOD * D_MODEL))


def _ffn_kernel(x_ref, mod_ref, g_ref, w1_ref, w3_ref, w2_ref, *rest, mod_base, final):
    if final:
        fg_ref, o_ref = rest
    else:
        (o_ref,) = rest
    x = x_ref[0]
    shift = mod_ref[0, mod_base:mod_base + 1, :]
    scale = mod_ref[0, mod_base + 1:mod_base + 2, :]
    gate = mod_ref[0, mod_base + 2:mod_base + 3, :]
    h = _rms_mod(x, g_ref[...], shift, scale).astype(BF16)
    a = jnp.dot(h, w1_ref[...], preferred_element_type=F32)
    b = jnp.dot(h, w3_ref[...], preferred_element_type=F32)
    u = (_silu(a) * b).astype(BF16)
    y = x + 0.5 * gate * jnp.dot(u, w2_ref[...], preferred_element_type=F32)
    if final:
        y = y * lax.rsqrt(jnp.mean(y * y, axis=-1, keepdims=True) + EPS) * fg_ref[...]
    o_ref[0] = y


def _ffn(x, mod, norm_g, w1, w3, w2, *, mod_base, shared_mod, final_g=None, tm=256):
    bsz, seq, _ = x.shape
    mod_map = (lambda b, t: (0, 0, 0)) if shared_mod else (lambda b, t: (b, 0, 0))
    in_specs = [pl.BlockSpec((1, tm, D_MODEL), lambda b, t: (b, t, 0)),
                pl.BlockSpec((1, N_MOD, D_MODEL), mod_map),
                _const_spec((1, D_MODEL)),
                _const_spec((D_MODEL, D_FF)),
                _const_spec((D_MODEL, D_FF)),
                _const_spec((D_FF, D_MODEL))]
    args = [x, mod, norm_g.reshape(1, D_MODEL), w1, w3, w2]
    if final_g is not None:
        in_specs.append(_const_spec((1, D_MODEL)))
        args.append(final_g.reshape(1, D_MODEL))
    return pl.pallas_call(
        functools.partial(_ffn_kernel, mod_base=mod_base, final=final_g is not None),
        grid=(bsz, seq // tm),
        in_specs=in_specs,
        out_specs=pl.BlockSpec((1, tm, D_MODEL), lambda b, t: (b, t, 0)),
        out_shape=jax.ShapeDtypeStruct(x.shape, F32),
        compiler_params=_cparams(2),
        name="swiglu_half_step",
    )(*args)


def _rope(t, cos, sin_lo, sin_hi):
    return (t * cos + pltpu.roll(t, HEAD_DIM - 32, 1) * sin_lo + pltpu.roll(t, 32, 1) * sin_hi)


def _inproj_kernel(x_ref, mod_ref, g_ref, w_ref, *rest, rope):
    if rope:
        cos_ref, slo_ref, shi_ref, pz_ref, q_ref, k_ref, v_ref, go_ref = rest
    else:
        pz_ref, q_ref, k_ref, v_ref, go_ref = rest
    x = x_ref[0]
    h = _rms_mod(x, g_ref[...], mod_ref[0, 3:4, :], mod_ref[0, 4:5, :]).astype(BF16)
    p = jnp.dot(h, w_ref[...], preferred_element_type=F32)
    k_scale = HEAD_DIM ** -0.5
    for hd in range(RET_HEADS):
        lo = hd * HEAD_DIM
        qh = p[:, Q_OFF + lo:Q_OFF + lo + HEAD_DIM]
        kh = p[:, K_OFF + lo:K_OFF + lo + HEAD_DIM]
        if rope:
            cos, slo, shi = cos_ref[...], slo_ref[...], shi_ref[...]
            qh = _rope(qh, cos, slo, shi)
            kh = _rope(kh, cos, slo, shi)
        q_ref[0, :, lo:lo + HEAD_DIM] = qh.astype(BF16)
        k_ref[0, :, lo:lo + HEAD_DIM] = (kh * k_scale).astype(BF16)
    v_ref[0] = p[:, V_OFF:G_OFF].astype(BF16)
    go_ref[0] = p[:, G_OFF:C_OFF].astype(BF16)
    pz_ref[0, :, :POOL_WIDTH] = p[:, :POOL_WIDTH]
    a = p[:, C_OFF:C_OFF + CONV_WIDTH]
    gt = p[:, C_OFF + CONV_WIDTH:]
    pz_ref[0, :, POOL_WIDTH:] = a * _sigmoid(gt)


def _inproj(x, mod, norm_g, w_in, rope_tabs, *, shared_mod, tm=256):
    bsz, seq, _ = x.shape
    mod_map = (lambda b, t: (0, 0, 0)) if shared_mod else (lambda b, t: (b, 0, 0))
    in_specs = [pl.BlockSpec((1, tm, D_MODEL), lambda b, t: (b, t, 0)),
                pl.BlockSpec((1, N_MOD, D_MODEL), mod_map),
                _const_spec((1, D_MODEL)),
                _const_spec((D_MODEL, IN_WIDTH))]
    args = [x, mod, norm_g.reshape(1, D_MODEL), w_in]
    if rope_tabs is not None:
        in_specs += [pl.BlockSpec((tm, HEAD_DIM), lambda b, t: (t, 0))] * 3
        args += list(rope_tabs)
    tok = lambda w: pl.BlockSpec((1, tm, w), lambda b, t: (b, t, 0))
    return pl.pallas_call(
        functools.partial(_inproj_kernel, rope=rope_tabs is not None),
        grid=(bsz, seq // tm),
        in_specs=in_specs,
        out_specs=[tok(PZ_WIDTH), tok(RET_WIDTH), tok(RET_WIDTH), tok(RET_WIDTH), tok(RET_WIDTH)],
        out_shape=[jax.ShapeDtypeStruct((bsz, seq, PZ_WIDTH), F32)]
        + [jax.ShapeDtypeStruct((bsz, seq, RET_WIDTH), BF16)] * 4,
        compiler_params=_cparams(2),
        name="mixer_in_proj",
    )(*args)


def _scan_kernel(kx_ref, vx_ref, ky_ref, vy_ref, wk_ref, cd_ref, sx_ref, sy_ref, st_ref):
    n_x = kx_ref.shape[1] // CHUNK
    n_y = ky_ref.shape[1] // CHUNK

    def run(direction, k_ref, v_ref, s_ref, n_chunks):
        def body(i, carry):
            n = i if direction == 0 else n_chunks - 1 - i
            rows = pl.ds(pl.multiple_of(n * CHUNK, CHUNK), CHUNK)
            for hd in range(RET_HEADS):
                lanes = slice(hd * HEAD_DIM, (hd + 1) * HEAD_DIM)
                s = st_ref[hd]
                s_ref[0, n, hd, :, direction * HEAD_DIM:(direction + 1) * HEAD_DIM] = s.astype(BF16)
                kc = k_ref[0, rows, lanes]
                vw = (v_ref[0, rows, lanes].astype(F32) * wk_ref[direction, :, lanes]).astype(BF16)
                kv = lax.dot_general(kc, vw, (((0,), (0,)), ((), ())), preferred_element_type=F32)
                st_ref[hd] = cd_ref[direction:direction + 1, lanes] * s + kv
            return carry
        lax.fori_loop(0, n_chunks, body, 0)

    for direction in (0, 1):
        st_ref[...] = jnp.zeros_like(st_ref)
        run(direction, ky_ref, vy_ref, sy_ref, n_y)
        run(direction, kx_ref, vx_ref, sx_ref, n_x)


def _state_scan(kx, vx, ky, vy, wk, cd):
    bsz, seq, _ = kx.shape
    ctx = ky.shape[1]
    tok = lambda n: pl.BlockSpec((1, n, RET_WIDTH), lambda b: (b, 0, 0))
    st = lambda n: pl.BlockSpec((1, n, RET_HEADS, HEAD_DIM, 2 * HEAD_DIM), lambda b: (b, 0, 0, 0, 0))
    return pl.pallas_call(
        _scan_kernel,
        grid=(bsz,),
        in_specs=[tok(seq), tok(seq), tok(ctx), tok(ctx),
                  _const_spec((2, CHUNK, RET_WIDTH)), _const_spec((2, RET_WIDTH))],
        out_specs=[st(seq // CHUNK), st(ctx // CHUNK)],
        out_shape=[jax.ShapeDtypeStruct((bsz, seq // CHUNK, RET_HEADS, HEAD_DIM, 2 * HEAD_DIM), BF16),
                   jax.ShapeDtypeStruct((bsz, ctx // CHUNK, RET_HEADS, HEAD_DIM, 2 * HEAD_DIM), BF16)],
        scratch_shapes=[pltpu.VMEM((RET_HEADS, HEAD_DIM, HEAD_DIM), F32)],
        compiler_params=_cparams(1),
        name="retention_state_scan",
    )(kx, vx, ky, vy, wk, cd)


def _mix_kernel(x_ref, mod_ref, pz_ref, pzp_ref, pzn_ref, q_ref, k_ref, v_ref, g_ref, s_ref,
                dmask_ref, wq_ref, poolw_ref, pscale_ref, gng_ref, dw_ref, db_ref, lng_ref, lnb_ref,
                wout_ref, o_ref, ext_ref, cat_ref, *, seq_len, tb):
    t = pl.program_id(1)
    n_t = pl.num_programs(1)

    ext_ref[0:HALO, :] = jnp.where(t > 0, pzp_ref[0], 0.0)
    ext_ref[HALO:HALO + tb, :] = pz_ref[0]
    ext_ref[HALO + tb:, :] = jnp.where(t < n_t - 1, pzn_ref[0], 0.0)

    def win(col, s):
        return ext_ref[HALO + s:HALO + s + tb, col * 128:(col + 1) * 128]

    lane = lax.broadcasted_iota(jnp.int32, (tb, 128), 1)
    pos = lax.broadcasted_iota(jnp.int32, (tb, 128), 0) + t * tb
    low_half = lane < POOL_GROUP
    pooled = []
    for col, (w_lo, w_hi) in enumerate(((2, 4), (8, 16))):
        acc = win(col, -w_lo // 2)
        for s in range(-w_lo // 2 + 1, w_lo // 2):
            acc = acc + win(col, s)
        wide = acc
        for s in list(range(-w_hi // 2, -w_lo // 2)) + list(range(w_lo // 2, w_hi // 2)):
            wide = wide + win(col, s)
        half = jnp.where(low_half, w_lo // 2, w_hi // 2)
        cnt = jnp.minimum(pos + half, seq_len) - jnp.maximum(pos - half, 0)
        pooled.append(jnp.where(low_half, acc, wide) / cnt.astype(F32) - win(col, 0))
    pooled = jnp.concatenate(pooled, axis=-1).astype(BF16)
    mixed = jnp.dot(pooled, poolw_ref[...], preferred_element_type=F32) * pscale_ref[...]
    cat_ref[:, :POOL_WIDTH] = mixed.astype(BF16)

    rb = 32
    for r0 in range(0, tb, rb):
        acc = jnp.zeros((rb, CONV_WIDTH), F32) + db_ref[...]
        for kk in range(CONV_K):
            lo = r0 + kk + HALO - CONV_K // 2
            acc = acc + dw_ref[kk:kk + 1, :] * ext_ref[lo:lo + rb, POOL_WIDTH:]
        mu = jnp.mean(acc, axis=-1, keepdims=True)
        cen = acc - mu
        var = jnp.mean(cen * cen, axis=-1, keepdims=True)
        zn = cen * lax.rsqrt(var + EPS) * lng_ref[...] + lnb_ref[...]
        cat_ref[r0:r0 + rb, POOL_WIDTH + RET_WIDTH:] = _silu(zn).astype(BF16)

    for c in range(tb // CHUNK):
        rows = slice(c * CHUNK, (c + 1) * CHUNK)
        for hd in range(RET_HEADS):
            lanes = slice(hd * HEAD_DIM, (hd + 1) * HEAD_DIM)
            qh = q_ref[0, rows, lanes]
            kh = k_ref[0, rows, lanes]
            vh = v_ref[0, rows, lanes]
            sc = lax.dot_general(qh, kh, (((1,), (1,)), ((), ())), preferred_element_type=F32)
            prob = (sc * dmask_ref[hd]).astype(BF16)
            o = jnp.dot(prob, vh, preferred_element_type=F32)
            cross = jnp.dot(qh, s_ref[0, c, hd], preferred_element_type=F32) * wq_ref[hd]
            o = o + cross[:, :HEAD_DIM] + cross[:, HEAD_DIM:]
            mu = jnp.mean(o, axis=-1, keepdims=True)
            cen = o - mu
            var = jnp.mean(cen * cen, axis=-1, keepdims=True)
            y = cen * lax.rsqrt(var + EPS) * gng_ref[:, lanes]
            y = y * _silu(g_ref[0, rows, lanes].astype(F32))
            cat_ref[rows, POOL_WIDTH + hd * HEAD_DIM:POOL_WIDTH + (hd + 1) * HEAD_DIM] = y.astype(BF16)

    out = jnp.dot(cat_ref[...], wout_ref[...], preferred_element_type=F32)
    o_ref[0] = x_ref[0] + mod_ref[0, 5:6, :] * out


def _mixer_body(x, mod, pz, q, k, v, g, states, dmask, wq, poolw, pscale, gng, dw, db, lng, lnb,
                w_out, *, shared_mod, tb=256):
    bsz, seq, _ = x.shape
    n_halo = seq // HALO
    hb = tb // HALO
    mod_map = (lambda b, t: (0, 0, 0)) if shared_mod else (lambda b, t: (b, 0, 0))
    tok = lambda w: pl.BlockSpec((1, tb, w), lambda b, t: (b, t, 0))
    in_specs = [
        tok(D_MODEL),
        pl.BlockSpec((1, N_MOD, D_MODEL), mod_map),
        tok(PZ_WIDTH),
        pl.BlockSpec((1, HALO, PZ_WIDTH), lambda b, t: (b, jnp.maximum(t * hb - 1, 0), 0)),
        pl.BlockSpec((1, HALO, PZ_WIDTH), lambda b, t: (b, jnp.minimum((t + 1) * hb, n_halo - 1), 0)),
        tok(RET_WIDTH), tok(RET_WIDTH), tok(RET_WIDTH), tok(RET_WIDTH),
        pl.BlockSpec((1, tb // CHUNK, RET_HEADS, HEAD_DIM, 2 * HEAD_DIM), lambda b, t: (b, t, 0, 0, 0)),
        _const_spec((RET_HEADS, CHUNK, CHUNK)),
        _const_spec((RET_HEADS, CHUNK, 2 * HEAD_DIM)),
        _const_spec((POOL_WIDTH, POOL_WIDTH)),
        _const_spec((1, POOL_WIDTH)),
        _const_spec((1, RET_WIDTH)),
        _const_spec((CONV_K + 1, CONV_WIDTH)),
        _const_spec((1, CONV_WIDTH)),
        _const_spec((1, CONV_WIDTH)),
        _const_spec((1, CONV_WIDTH)),
        _const_spec((D_MODEL, D_MODEL)),
    ]
    return pl.pallas_call(
        functools.partial(_mix_kernel, seq_len=seq, tb=tb),
        grid=(bsz, seq // tb),
        in_specs=in_specs,
        out_specs=tok(D_MODEL),
        out_shape=jax.ShapeDtypeStruct(x.shape, F32),
        scratch_shapes=[pltpu.VMEM((tb + 2 * HALO, PZ_WIDTH), F32),
                        pltpu.VMEM((tb, D_MODEL), BF16)],
        compiler_params=_cparams(2),
        name="mixer_body",
    )(x, mod, pz, pz, pz, q, k, v, g, states, dmask, wq, poolw, pscale, gng, dw, db, lng, lnb, w_out)


def _rope_tables(seq):
    n_freq = HEAD_DIM // 4
    inv = ROPE_BASE ** (-jnp.arange(n_freq, dtype=F32) / n_freq)
    rows = seq // GRID_W
    row = jnp.repeat(jnp.arange(rows, dtype=F32), GRID_W)
    col = jnp.tile(jnp.arange(GRID_W, dtype=F32), rows)
    ang_r = row[:, None] * inv[None]
    ang_c = col[:, None] * inv[None]
    zeros = jnp.zeros_like(ang_r)
    cos = jnp.concatenate([jnp.cos(ang_r)] * 2 + [jnp.cos(ang_c)] * 2, axis=-1)
    sin_lo = jnp.concatenate([-jnp.sin(ang_r), zeros, -jnp.sin(ang_c), zeros], axis=-1)
    sin_hi = jnp.concatenate([zeros, jnp.sin(ang_r), zeros, jnp.sin(ang_c)], axis=-1)
    return cos, sin_lo, sin_hi


def _retention_tables(dec_f, dec_b):
    lg_f = jax.nn.log_sigmoid(dec_f.astype(F32))
    lg_b = jax.nn.log_sigmoid(dec_b.astype(F32))
    pos = jnp.arange(CHUNK, dtype=F32)
    diff = pos[:, None] - pos[None, :]
    dmask = jnp.where(diff[None] >= 0,
                      jnp.exp(jnp.maximum(diff, 0.0)[None] * lg_f[:, None, None]),
                      jnp.exp(jnp.maximum(-diff, 0.0)[None] * lg_b[:, None, None]))
    ones = jnp.ones((1, 1, HEAD_DIM), F32)
    wq_f = jnp.exp((pos + 1.0)[None, :] * lg_f[:, None])[:, :, None] * ones
    wq_b = jnp.exp((CHUNK - pos)[None, :] * lg_b[:, None])[:, :, None] * ones
    wq = jnp.concatenate([wq_f, wq_b], axis=-1)
    wk_f = jnp.exp((CHUNK - 1 - pos)[:, None] * lg_f[None, :])
    wk_b = jnp.exp(pos[:, None] * lg_b[None, :])
    wk = jnp.stack([jnp.repeat(wk_f, HEAD_DIM, axis=1), jnp.repeat(wk_b, HEAD_DIM, axis=1)])
    cd = jnp.stack([jnp.repeat(jnp.exp(CHUNK * lg_f), HEAD_DIM),
                    jnp.repeat(jnp.exp(CHUNK * lg_b), HEAD_DIM)])
    return dmask, wq, wk, cd


def _block_diag(pool_w):
    out = jnp.zeros((POOL_WIDTH, POOL_WIDTH), pool_w.dtype)
    for gi in range(len(POOL_WINDOWS)):
        out = out.at[gi * POOL_GROUP:(gi + 1) * POOL_GROUP, gi * POOL_GROUP:(gi + 1) * POOL_GROUP].set(pool_w[gi])
    return out


def kernel(x, c, ctx, c_ctx, w_mod, b_mod, norm_g, ffn_w1, ffn_w3, ffn_w2, w_in, w_out, pool_w,
           pool_scale, ret_decay_fwd, ret_decay_bwd, ret_gn_g, conv_dw, conv_b, conv_ln_g,
           conv_ln_b, final_g):
    depth = w_mod.shape[0]
    bsz, seq, _ = x.shape
    cvec = jnp.zeros((MOD_ROWS, D_MODEL), F32).at[:bsz].set(c).at[bsz].set(c_ctx)
    mod_all = _modulation(cvec, w_mod, b_mod).reshape(depth, MOD_ROWS, N_MOD, D_MODEL)
    rope_tabs = _rope_tables(seq)
    y = ctx
    for l in range(depth):
        last = l == depth - 1
        mx = mod_all[l, :bsz]
        my = mod_all[l, bsz:bsz + 1]
        w1 = ffn_w1[l].astype(BF16)
        w3 = ffn_w3[l].astype(BF16)
        w2 = ffn_w2[l].astype(BF16)
        win = w_in[l].astype(BF16)
        wout = w_out[l].astype(BF16)

        x = _ffn(x, mx, norm_g[l, 0], w1[0], w3[0], w2[0], mod_base=0, shared_mod=False)
        y = _ffn(y, my, norm_g[l, 0], w1[0], w3[0], w2[0], mod_base=0, shared_mod=True)

        dmask, wq, wk, cd = _retention_tables(ret_decay_fwd[l], ret_decay_bwd[l])
        pzx, qx, kx, vx, gx = _inproj(x, mx, norm_g[l, 1], win, rope_tabs, shared_mod=False)
        pzy, qy, ky, vy, gy = _inproj(y, my, norm_g[l, 1], win, None, shared_mod=True)
        sx, sy = _state_scan(kx, vx, ky, vy, wk, cd)
        mix_params = (dmask, wq, _block_diag(pool_w[l]).astype(BF16), pool_scale[l].reshape(1, -1),
                      ret_gn_g[l].reshape(1, -1),
                      jnp.concatenate([conv_dw[l], jnp.zeros((1, CONV_WIDTH), F32)], axis=0),
                      conv_b[l].reshape(1, -1), conv_ln_g[l].reshape(1, -1), conv_ln_b[l].reshape(1, -1),
                      wout)
        x = _mixer_body(x, mx, pzx, qx, kx, vx, gx, sx, *mix_params, shared_mod=False)
        if not last:
            y = _mixer_body(y, my, pzy, qy, ky, vy, gy, sy, *mix_params, shared_mod=True)

        x = _ffn(x, mx, norm_g[l, 2], w1[1], w3[1], w2[1], mod_base=6, shared_mod=False,
                 final_g=final_g if last else None)
        if not last:
            y = _ffn(y, my, norm_g[l, 2], w1[1], w3[1], w2[1], mod_base=6, shared_mod=True)
    return x
```

```python
import functools

import jax
import jax.numpy as jnp
from jax import lax
from jax.experimental import pallas as pl
from jax.experimental.pallas import tpu as pltpu

F32 = jnp.float32
BF16 = jnp.bfloat16

D_MODEL = 1024
N_MOD = 9
D_FF = 2816
POOL_WIDTH = 256
POOL_WINDOWS = (2, 4, 8, 16)
POOL_GROUP = 64
RET_WIDTH = 512
RET_HEADS = 4
HEAD_DIM = 128
CHUNK = 128
CONV_WIDTH = 256
CONV_K = 31
IN_WIDTH = 2816
Q_OFF = POOL_WIDTH
K_OFF = Q_OFF + RET_WIDTH
V_OFF = K_OFF + RET_WIDTH
G_OFF = V_OFF + RET_WIDTH
C_OFF = G_OFF + RET_WIDTH
GRID_W = 64
ROPE_BASE = 10000.0
EPS = 1e-6

SUBLANES = 8
FFN_ROWS = 512
FFN_SUB_ROWS = 128
INPROJ_ROWS = 512
INPROJ_SUB_ROWS = 128
HALO = 16
PZ_WIDTH = POOL_WIDTH + CONV_WIDTH
MOD_ROWS = 24
VMEM_LIMIT = 56 * 1024 * 1024


def _cparams(n_axes):
    return pltpu.CompilerParams(dimension_semantics=("arbitrary",) * n_axes,
                                vmem_limit_bytes=VMEM_LIMIT)


def _const_spec(shape):
    nd = len(shape)
    return pl.BlockSpec(shape, lambda *_: (0,) * nd, pipeline_mode=pl.Buffered(1))


def _sigmoid(x):
    return 1.0 / (1.0 + jnp.exp(-x))


def _silu(x):
    return x * _sigmoid(x)


def _rms_mod(x, g, shift, scale):
    y = x * lax.rsqrt(jnp.mean(x * x, axis=-1, keepdims=True) + EPS) * g
    return y * (1.0 + scale) + shift


def _mod_kernel(c_ref, w_ref, b_ref, o_ref):
    s = _silu(c_ref[...]).astype(BF16)
    o_ref[0] = jnp.dot(s, w_ref[0].astype(BF16), preferred_element_type=F32) + b_ref[0]


def _modulation(cvec, w_mod, b_mod):
    depth = w_mod.shape[0]
    tn = 1152
    return pl.pallas_call(
        _mod_kernel,
        grid=(depth, N_MOD * D_MODEL // tn),
        in_specs=[pl.BlockSpec((MOD_ROWS, D_MODEL), lambda l, j: (0, 0)),
                  pl.BlockSpec((1, D_MODEL, tn), lambda l, j: (l, 0, j)),
                  pl.BlockSpec((1, 1, tn), lambda l, j: (l, 0, j))],
        out_specs=pl.BlockSpec((1, MOD_ROWS, tn), lambda l, j: (l, 0, j)),
        out_shape=jax.ShapeDtypeStruct((depth, MOD_ROWS, N_MOD * D_MODEL), F32),
        compiler_params=_cparams(2),
        name="modulation",
    )(cvec, w_mod, b_mod.reshape(depth, 1, N_MOD * D_MODEL))


def _ffn_kernel(x_ref, mod_ref, g_ref, w1_ref, w3_ref, w2_ref, *rest, mod_base, final, sub):
    if final:
        fg_ref, o_ref = rest
    else:
        (o_ref,) = rest
    shift = mod_ref[0, mod_base:mod_base + 1, :]
    scale = mod_ref[0, mod_base + 1:mod_base + 2, :]
    gate = mod_ref[0, mod_base + 2:mod_base + 3, :]
    for r0 in range(0, x_ref.shape[1], sub):
        rows = slice(r0, r0 + sub)
        x = x_ref[0, rows, :]
        h = _rms_mod(x, g_ref[...], shift, scale).astype(BF16)
        a = jnp.dot(h, w1_ref[...], preferred_element_type=F32)
        b = jnp.dot(h, w3_ref[...], preferred_element_type=F32)
        u = (_silu(a) * b).astype(BF16)
        y = x + 0.5 * gate * jnp.dot(u, w2_ref[...], preferred_element_type=F32)
        if final:
            y = y * lax.rsqrt(jnp.mean(y * y, axis=-1, keepdims=True) + EPS) * fg_ref[...]
        o_ref[0, rows, :] = y


def _ffn(x, mod, norm_g, w1, w3, w2, *, mod_base, shared_mod, final_g=None, tm=FFN_ROWS):
    out_shape = x.shape
    if shared_mod:
        x = x.reshape(1, -1, D_MODEL)
    bsz, seq, _ = x.shape
    mod_map = (lambda b, t: (0, 0, 0)) if shared_mod else (lambda b, t: (b, 0, 0))
    in_specs = [pl.BlockSpec((1, tm, D_MODEL), lambda b, t: (b, t, 0)),
                pl.BlockSpec((1, N_MOD, D_MODEL), mod_map),
                _const_spec((1, D_MODEL)),
                _const_spec((D_MODEL, D_FF)),
                _const_spec((D_MODEL, D_FF)),
                _const_spec((D_FF, D_MODEL))]
    args = [x, mod, norm_g.reshape(1, D_MODEL), w1, w3, w2]
    if final_g is not None:
        in_specs.append(_const_spec((1, D_MODEL)))
        args.append(final_g.reshape(1, D_MODEL))
    return pl.pallas_call(
        functools.partial(_ffn_kernel, mod_base=mod_base, final=final_g is not None, sub=FFN_SUB_ROWS),
        grid=(bsz, seq // tm),
        in_specs=in_specs,
        out_specs=pl.BlockSpec((1, tm, D_MODEL), lambda b, t: (b, t, 0)),
        out_shape=jax.ShapeDtypeStruct(x.shape, F32),
        compiler_params=_cparams(2),
        name="swiglu_half_step",
    )(*args).reshape(out_shape)


def _rope(t, cos, sin):
    return t * cos + pltpu.roll(t, HEAD_DIM // 2, 1) * sin


def _inproj_kernel(x_ref, mod_ref, g_ref, w_ref, *rest, rope, sub):
    if rope:
        cos_ref, sin_ref, pz_ref, q_ref, k_ref, v_ref, go_ref = rest
    else:
        pz_ref, q_ref, k_ref, v_ref, go_ref = rest
    k_scale = HEAD_DIM ** -0.5
    for r0 in range(0, x_ref.shape[1], sub):
        rows = slice(r0, r0 + sub)
        h = _rms_mod(x_ref[0, rows, :], g_ref[...], mod_ref[0, 3:4, :], mod_ref[0, 4:5, :]).astype(BF16)

        def proj(lo, hi):
            return jnp.dot(h, w_ref[:, lo:hi], preferred_element_type=F32)

        for off, dst, scale in ((Q_OFF, q_ref, None), (K_OFF, k_ref, k_scale)):
            t = proj(off, off + RET_WIDTH)
            for hd in range(RET_HEADS):
                lanes = slice(hd * HEAD_DIM, (hd + 1) * HEAD_DIM)
                th = t[:, lanes]
                if rope:
                    th = _rope(th, cos_ref[rows, :], sin_ref[rows, :])
                if scale is not None:
                    th = th * scale
                dst[0, rows, lanes] = th.astype(BF16)
        v_ref[0, rows, :] = proj(V_OFF, G_OFF).astype(BF16)
        go_ref[0, rows, :] = proj(G_OFF, C_OFF).astype(BF16)
        pz_ref[0, rows, :POOL_WIDTH] = proj(0, POOL_WIDTH)
        glu = proj(C_OFF, IN_WIDTH)
        pz_ref[0, rows, POOL_WIDTH:] = glu[:, :CONV_WIDTH] * _sigmoid(glu[:, CONV_WIDTH:])


def _inproj(x, mod, norm_g, w_in, rope_tabs, *, shared_mod, tm=INPROJ_ROWS):
    lead = x.shape[:2]
    if shared_mod:
        assert rope_tabs is None
        x = x.reshape(1, -1, D_MODEL)
    bsz, seq, _ = x.shape
    mod_map = (lambda b, t: (0, 0, 0)) if shared_mod else (lambda b, t: (b, 0, 0))
    in_specs = [pl.BlockSpec((1, tm, D_MODEL), lambda b, t: (b, t, 0)),
                pl.BlockSpec((1, N_MOD, D_MODEL), mod_map),
                _const_spec((1, D_MODEL)),
                _const_spec((D_MODEL, IN_WIDTH))]
    args = [x, mod, norm_g.reshape(1, D_MODEL), w_in]
    if rope_tabs is not None:
        in_specs += [pl.BlockSpec((tm, HEAD_DIM), lambda b, t: (t, 0))] * 2
        args += list(rope_tabs)
    tok = lambda w: pl.BlockSpec((1, tm, w), lambda b, t: (b, t, 0))
    outs = pl.pallas_call(
        functools.partial(_inproj_kernel, rope=rope_tabs is not None, sub=INPROJ_SUB_ROWS),
        grid=(bsz, seq // tm),
        in_specs=in_specs,
        out_specs=[tok(PZ_WIDTH), tok(RET_WIDTH), tok(RET_WIDTH), tok(RET_WIDTH), tok(RET_WIDTH)],
        out_shape=[jax.ShapeDtypeStruct((bsz, seq, PZ_WIDTH), F32)]
        + [jax.ShapeDtypeStruct((bsz, seq, RET_WIDTH), BF16)] * 4,
        compiler_params=_cparams(2),
        name="mixer_in_proj",
    )(*args)
    return [o.reshape(lead + o.shape[2:]) for o in outs]


def _scan_kernel(kx_ref, vx_ref, ky_ref, vy_ref, wk_ref, cd_ref, sx_ref, sy_ref, kvx_ref, kvy_ref):
    n_x = kx_ref.shape[1] // CHUNK
    n_y = ky_ref.shape[1] // CHUNK

    def increments(k_ref, v_ref, kv_ref, n_chunks):
        def body(n, carry):
            rows = pl.ds(pl.multiple_of(n * CHUNK, CHUNK), CHUNK)
            for hd in range(RET_HEADS):
                lanes = slice(hd * HEAD_DIM, (hd + 1) * HEAD_DIM)
                v = v_ref[0, rows, lanes].astype(F32)
                vw = (jnp.concatenate([v, v], axis=-1) * wk_ref[hd]).astype(BF16)
                kv_ref[n, hd] = lax.dot_general(k_ref[0, rows, lanes], vw, (((0,), (0,)), ((), ())),
                                                preferred_element_type=F32)
            return carry
        lax.fori_loop(0, n_chunks, body, 0)

    increments(ky_ref, vy_ref, kvy_ref, n_y)
    increments(kx_ref, vx_ref, kvx_ref, n_x)

    for hd in range(RET_HEADS):
        for direction in (0, 1):
            lanes = slice(direction * HEAD_DIM, (direction + 1) * HEAD_DIM)
            decay = cd_ref[direction:direction + 1, hd * HEAD_DIM:(hd + 1) * HEAD_DIM]

            def scan(state, kv_ref, s_ref, n_chunks):
                def body(i, st):
                    n = i if direction == 0 else n_chunks - 1 - i
                    s_ref[0, n, hd, :, lanes] = st.astype(BF16)
                    return decay * st + kv_ref[n, hd, :, lanes]
                return lax.fori_loop(0, n_chunks, body, state)

            state = scan(jnp.zeros((HEAD_DIM, HEAD_DIM), F32), kvy_ref, sy_ref, n_y)
            scan(state, kvx_ref, sx_ref, n_x)


def _state_scan(kx, vx, ky, vy, wk, cd):
    bsz, seq, _ = kx.shape
    ctx = ky.shape[1]
    tok = lambda n: pl.BlockSpec((1, n, RET_WIDTH), lambda b: (b, 0, 0))
    st = lambda n: pl.BlockSpec((1, n, RET_HEADS, HEAD_DIM, 2 * HEAD_DIM), lambda b: (b, 0, 0, 0, 0))
    return pl.pallas_call(
        _scan_kernel,
        grid=(bsz,),
        in_specs=[tok(seq), tok(seq), tok(ctx), tok(ctx),
                  _const_spec((RET_HEADS, CHUNK, 2 * HEAD_DIM)), _const_spec((2, RET_WIDTH))],
        out_specs=[st(seq // CHUNK), st(ctx // CHUNK)],
        out_shape=[jax.ShapeDtypeStruct((bsz, seq // CHUNK, RET_HEADS, HEAD_DIM, 2 * HEAD_DIM), BF16),
                   jax.ShapeDtypeStruct((bsz, ctx // CHUNK, RET_HEADS, HEAD_DIM, 2 * HEAD_DIM), BF16)],
        scratch_shapes=[pltpu.VMEM((seq // CHUNK, RET_HEADS, HEAD_DIM, 2 * HEAD_DIM), F32),
                        pltpu.VMEM((ctx // CHUNK, RET_HEADS, HEAD_DIM, 2 * HEAD_DIM), F32)],
        compiler_params=_cparams(1),
        name="retention_state_scan",
    )(kx, vx, ky, vy, wk, cd)


def _mix_kernel(x_ref, mod_ref, pz_ref, pzp_ref, pzn_ref, q_ref, k_ref, v_ref, g_ref, s_ref,
                dmask_ref, wq_ref, poolw_ref, pscale_ref, gng_ref, dw_ref, db_ref, lng_ref, lnb_ref,
                wout_ref, o_ref, ext_ref, cat_ref, sh_ref, *, seq_len, tb):
    t = pl.program_id(1)
    n_t = pl.num_programs(1)

    ext_ref[0:HALO, :] = jnp.where(t > 0, pzp_ref[0], 0.0)
    ext_ref[HALO:HALO + tb, :] = pz_ref[0]
    ext_ref[HALO + tb:, :] = jnp.where(t < n_t - 1, pzn_ref[0], 0.0)

    def win(col, s):
        return ext_ref[HALO + s:HALO + s + tb, col * 128:(col + 1) * 128]

    lane = lax.broadcasted_iota(jnp.int32, (tb, 128), 1)
    pos = lax.broadcasted_iota(jnp.int32, (tb, 128), 0) + t * tb
    low_half = lane < POOL_GROUP
    pooled = []
    for col, (w_lo, w_hi) in enumerate(((2, 4), (8, 16))):
        acc = win(col, -w_lo // 2)
        for s in range(-w_lo // 2 + 1, w_lo // 2):
            acc = acc + win(col, s)
        wide = acc
        for s in list(range(-w_hi // 2, -w_lo // 2)) + list(range(w_lo // 2, w_hi // 2)):
            wide = wide + win(col, s)
        half = jnp.where(low_half, w_lo // 2, w_hi // 2)
        cnt = jnp.minimum(pos + half, seq_len) - jnp.maximum(pos - half, 0)
        pooled.append(jnp.where(low_half, acc, wide) / cnt.astype(F32) - win(col, 0))
    pooled = jnp.concatenate(pooled, axis=-1).astype(BF16)
    mixed = jnp.dot(pooled, poolw_ref[...], preferred_element_type=F32) * pscale_ref[...]
    cat_ref[:, :POOL_WIDTH] = mixed.astype(BF16)

    n_sh = sh_ref.shape[1]
    for r in range(1, SUBLANES):
        sh_ref[r - 1] = ext_ref[r:r + n_sh, POOL_WIDTH:]
    rb = 64
    for r0 in range(0, tb, rb):
        acc = jnp.zeros((rb, CONV_WIDTH), F32) + db_ref[...]
        for kk in range(CONV_K):
            lo = r0 + kk + HALO - CONV_K // 2
            r = lo % SUBLANES
            if r == 0:
                src = ext_ref[lo:lo + rb, POOL_WIDTH:]
            else:
                src = sh_ref[r - 1, lo - r:lo - r + rb, :]
            acc = acc + dw_ref[kk:kk + 1, :] * src
        mu = jnp.mean(acc, axis=-1, keepdims=True)
        cen = acc - mu
        var = jnp.mean(cen * cen, axis=-1, keepdims=True)
        zn = cen * lax.rsqrt(var + EPS) * lng_ref[...] + lnb_ref[...]
        cat_ref[r0:r0 + rb, POOL_WIDTH + RET_WIDTH:] = _silu(zn).astype(BF16)

    for c in range(tb // CHUNK):
        rows = slice(c * CHUNK, (c + 1) * CHUNK)
        for hd in range(RET_HEADS):
            lanes = slice(hd * HEAD_DIM, (hd + 1) * HEAD_DIM)
            qh = q_ref[0, rows, lanes]
            kh = k_ref[0, rows, lanes]
            vh = v_ref[0, rows, lanes]
            sc = lax.dot_general(qh, kh, (((1,), (1,)), ((), ())), preferred_element_type=F32)
            prob = (sc * dmask_ref[hd]).astype(BF16)
            o = jnp.dot(prob, vh, preferred_element_type=F32)
            cross = jnp.dot(qh, s_ref[0, c, hd], preferred_element_type=F32) * wq_ref[hd]
            o = o + cross[:, :HEAD_DIM] + cross[:, HEAD_DIM:]
            mu = jnp.mean(o, axis=-1, keepdims=True)
            cen = o - mu
            var = jnp.mean(cen * cen, axis=-1, keepdims=True)
            y = cen * lax.rsqrt(var + EPS) * gng_ref[:, lanes]
            y = y * _silu(g_ref[0, rows, lanes].astype(F32))
            cat_ref[rows, POOL_WIDTH + hd * HEAD_DIM:POOL_WIDTH + (hd + 1) * HEAD_DIM] = y.astype(BF16)

    out = jnp.dot(cat_ref[...], wout_ref[...], preferred_element_type=F32)
    o_ref[0] = x_ref[0] + mod_ref[0, 5:6, :] * out


def _mixer_body(x, mod, pz, q, k, v, g, states, dmask, wq, poolw, pscale, gng, dw, db, lng, lnb,
                w_out, *, shared_mod, tb=256):
    bsz, seq, _ = x.shape
    n_halo = seq // HALO
    hb = tb // HALO
    mod_map = (lambda b, t: (0, 0, 0)) if shared_mod else (lambda b, t: (b, 0, 0))
    tok = lambda w: pl.BlockSpec((1, tb, w), lambda b, t: (b, t, 0))
    in_specs = [
        tok(D_MODEL),
        pl.BlockSpec((1, N_MOD, D_MODEL), mod_map),
        tok(PZ_WIDTH),
        pl.BlockSpec((1, HALO, PZ_WIDTH), lambda b, t: (b, jnp.maximum(t * hb - 1, 0), 0)),
        pl.BlockSpec((1, HALO, PZ_WIDTH), lambda b, t: (b, jnp.minimum((t + 1) * hb, n_halo - 1), 0)),
        tok(RET_WIDTH), tok(RET_WIDTH), tok(RET_WIDTH), tok(RET_WIDTH),
        pl.BlockSpec((1, tb // CHUNK, RET_HEADS, HEAD_DIM, 2 * HEAD_DIM), lambda b, t: (b, t, 0, 0, 0)),
        _const_spec((RET_HEADS, CHUNK, CHUNK)),
        _const_spec((RET_HEADS, CHUNK, 2 * HEAD_DIM)),
        _const_spec((POOL_WIDTH, POOL_WIDTH)),
        _const_spec((1, POOL_WIDTH)),
        _const_spec((1, RET_WIDTH)),
        _const_spec((CONV_K + 1, CONV_WIDTH)),
        _const_spec((1, CONV_WIDTH)),
        _const_spec((1, CONV_WIDTH)),
        _const_spec((1, CONV_WIDTH)),
        _const_spec((D_MODEL, D_MODEL)),
    ]
    return pl.pallas_call(
        functools.partial(_mix_kernel, seq_len=seq, tb=tb),
        grid=(bsz, seq // tb),
        in_specs=in_specs,
        out_specs=tok(D_MODEL),
        out_shape=jax.ShapeDtypeStruct(x.shape, F32),
        scratch_shapes=[pltpu.VMEM((tb + 2 * HALO, PZ_WIDTH), F32),
                        pltpu.VMEM((tb, D_MODEL), BF16),
                        pltpu.VMEM((SUBLANES - 1, tb + 2 * HALO - SUBLANES, CONV_WIDTH), F32)],
        compiler_params=_cparams(2),
        name="mixer_body",
    )(x, mod, pz, pz, pz, q, k, v, g, states, dmask, wq, poolw, pscale, gng, dw, db, lng, lnb, w_out)


def _rope_tables(seq):
    n_freq = HEAD_DIM // 4
    inv = ROPE_BASE ** (-jnp.arange(n_freq, dtype=F32) / n_freq)
    rows = seq // GRID_W
    row = jnp.repeat(jnp.arange(rows, dtype=F32), GRID_W)
    col = jnp.tile(jnp.arange(GRID_W, dtype=F32), rows)
    ang_r = row[:, None] * inv[None]
    ang_c = col[:, None] * inv[None]
    cos = jnp.concatenate([jnp.cos(ang_r), jnp.cos(ang_c)] * 2, axis=-1)
    sin = jnp.concatenate([-jnp.sin(ang_r), -jnp.sin(ang_c), jnp.sin(ang_r), jnp.sin(ang_c)], axis=-1)
    return cos, sin


def _permute_heads(w_in):
    def perm(w):
        w = w.reshape(D_MODEL, RET_HEADS, 2, 2, HEAD_DIM // 4)
        return w.transpose(0, 1, 3, 2, 4).reshape(D_MODEL, RET_WIDTH)
    return jnp.concatenate([w_in[:, :Q_OFF], perm(w_in[:, Q_OFF:K_OFF]), perm(w_in[:, K_OFF:V_OFF]),
                            w_in[:, V_OFF:]], axis=1)


def _retention_tables(dec_f, dec_b):
    lg_f = jax.nn.log_sigmoid(dec_f.astype(F32))
    lg_b = jax.nn.log_sigmoid(dec_b.astype(F32))
    pos = jnp.arange(CHUNK, dtype=F32)
    diff = pos[:, None] - pos[None, :]
    dmask = jnp.where(diff[None] >= 0,
                      jnp.exp(jnp.maximum(diff, 0.0)[None] * lg_f[:, None, None]),
                      jnp.exp(jnp.maximum(-diff, 0.0)[None] * lg_b[:, None, None]))
    ones = jnp.ones((1, 1, HEAD_DIM), F32)
    wq_f = jnp.exp((pos + 1.0)[None, :] * lg_f[:, None])[:, :, None] * ones
    wq_b = jnp.exp((CHUNK - pos)[None, :] * lg_b[:, None])[:, :, None] * ones
    wq = jnp.concatenate([wq_f, wq_b], axis=-1)
    wk_f = jnp.exp((CHUNK - 1 - pos)[None, :] * lg_f[:, None])[:, :, None] * ones
    wk_b = jnp.exp(pos[None, :] * lg_b[:, None])[:, :, None] * ones
    wk = jnp.concatenate([wk_f, wk_b], axis=-1)
    cd = jnp.stack([jnp.repeat(jnp.exp(CHUNK * lg_f), HEAD_DIM),
                    jnp.repeat(jnp.exp(CHUNK * lg_b), HEAD_DIM)])
    return dmask, wq, wk, cd


def _block_diag(pool_w):
    out = jnp.zeros((POOL_WIDTH, POOL_WIDTH), pool_w.dtype)
    for gi in range(len(POOL_WINDOWS)):
        out = out.at[gi * POOL_GROUP:(gi + 1) * POOL_GROUP, gi * POOL_GROUP:(gi + 1) * POOL_GROUP].set(pool_w[gi])
    return out


def kernel(x, c, ctx, c_ctx, w_mod, b_mod, norm_g, ffn_w1, ffn_w3, ffn_w2, w_in, w_out, pool_w,
           pool_scale, ret_decay_fwd, ret_decay_bwd, ret_gn_g, conv_dw, conv_b, conv_ln_g,
           conv_ln_b, final_g):
    depth = w_mod.shape[0]
    bsz, seq, _ = x.shape
    cvec = jnp.zeros((MOD_ROWS, D_MODEL), F32).at[:bsz].set(c).at[bsz].set(c_ctx)
    mod_all = _modulation(cvec, w_mod, b_mod).reshape(depth, MOD_ROWS, N_MOD, D_MODEL)
    rope_tabs = _rope_tables(seq)
    y = ctx
    for l in range(depth):
        last = l == depth - 1
        mx = mod_all[l, :bsz]
        my = mod_all[l, bsz:bsz + 1]
        w1 = ffn_w1[l].astype(BF16)
        w3 = ffn_w3[l].astype(BF16)
        w2 = ffn_w2[l].astype(BF16)
        win = _permute_heads(w_in[l]).astype(BF16)
        wout = w_out[l].astype(BF16)

        x = _ffn(x, mx, norm_g[l, 0], w1[0], w3[0], w2[0], mod_base=0, shared_mod=False)
        y = _ffn(y, my, norm_g[l, 0], w1[0], w3[0], w2[0], mod_base=0, shared_mod=True)

        dmask, wq, wk, cd = _retention_tables(ret_decay_fwd[l], ret_decay_bwd[l])
        pzx, qx, kx, vx, gx = _inproj(x, mx, norm_g[l, 1], win, rope_tabs, shared_mod=False)
        pzy, qy, ky, vy, gy = _inproj(y, my, norm_g[l, 1], win, None, shared_mod=True)
        sx, sy = _state_scan(kx, vx, ky, vy, wk, cd)
        mix_params = (dmask, wq, _block_diag(pool_w[l]).astype(BF16), pool_scale[l].reshape(1, -1),
                      ret_gn_g[l].reshape(1, -1),
                      jnp.concatenate([conv_dw[l], jnp.zeros((1, CONV_WIDTH), F32)], axis=0),
                      conv_b[l].reshape(1, -1), conv_ln_g[l].reshape(1, -1), conv_ln_b[l].reshape(1, -1),
                      wout)
        x = _mixer_body(x, mx, pzx, qx, kx, vx, gx, sx, *mix_params, shared_mod=False)
        if not last:
            y = _mixer_body(y, my, pzy, qy, ky, vy, gy, sy, *mix_params, shared_mod=True)

        x = _ffn(x, mx, norm_g[l, 2], w1[1], w3[1], w2[1], mod_base=6, shared_mod=False,
                 final_g=final_g if last else None)
        if not last:
            y = _ffn(y, my, norm_g[l, 2], w1[1], w3[1], w2[1], mod_base=6, shared_mod=True)
    return x
```

```python
import functools

import jax
import jax.numpy as jnp
from jax import lax
from jax.experimental import pallas as pl
from jax.experimental.pallas import tpu as pltpu

F32 = jnp.float32
BF16 = jnp.bfloat16

D_MODEL = 1024
N_MOD = 9
D_FF = 2816
POOL_WIDTH = 256
POOL_WINDOWS = (2, 4, 8, 16)
POOL_GROUP = 64
RET_WIDTH = 512
RET_HEADS = 4
HEAD_DIM = 128
CHUNK = 128
CONV_WIDTH = 256
CONV_K = 31
IN_WIDTH = 2816
Q_OFF = POOL_WIDTH
K_OFF = Q_OFF + RET_WIDTH
V_OFF = K_OFF + RET_WIDTH
G_OFF = V_OFF + RET_WIDTH
C_OFF = G_OFF + RET_WIDTH
GRID_W = 64
ROPE_BASE = 10000.0
EPS = 1e-6

SUBLANES = 8
LANES = 128
FFN_ROWS = 1024
FFN_SUB_ROWS = 128
INPROJ_ROWS = 1024
INPROJ_SUB_ROWS = 128
MIX_ROWS = 512
SCAN_UNROLL = 2
HALO = 16
PZ_WIDTH = POOL_WIDTH + CONV_WIDTH
MOD_ROWS = 24
VMEM_LIMIT = 56 * 1024 * 1024


def _cparams(n_axes):
    return pltpu.CompilerParams(dimension_semantics=("arbitrary",) * n_axes,
                                vmem_limit_bytes=VMEM_LIMIT)


def _const_spec(shape, lead=()):
    idx = tuple(lead) + (0,) * len(shape)
    return pl.BlockSpec((None,) * len(lead) + tuple(shape), lambda *_: idx,
                        pipeline_mode=pl.Buffered(1))


def _sigmoid(x):
    return 1.0 / (1.0 + jnp.exp(-x))


def _silu(x):
    return x * _sigmoid(x)


def _rms_mod(x, g, shift, scale):
    y = x * lax.rsqrt(jnp.mean(x * x, axis=-1, keepdims=True) + EPS) * g
    return y * (1.0 + scale) + shift


def _swiglu_rows(x, g, mod_ref, mod_base, w1_ref, w3_ref, w2_ref):
    shift = mod_ref[mod_base:mod_base + 1, :]
    scale = mod_ref[mod_base + 1:mod_base + 2, :]
    gate = mod_ref[mod_base + 2:mod_base + 3, :]
    h = _rms_mod(x, g, shift, scale).astype(BF16)
    a = jnp.dot(h, w1_ref[...], preferred_element_type=F32)
    b = jnp.dot(h, w3_ref[...], preferred_element_type=F32)
    u = (_silu(a) * b).astype(BF16)
    return x + 0.5 * gate * jnp.dot(u, w2_ref[...], preferred_element_type=F32)


def _mod_kernel(c_ref, w_ref, b_ref, o_ref):
    s = _silu(c_ref[...]).astype(BF16)
    o_ref[0] = jnp.dot(s, w_ref[0].astype(BF16), preferred_element_type=F32) + b_ref[0]


def _modulation(cvec, w_mod, b_mod):
    depth = w_mod.shape[0]
    tn = 1152
    return pl.pallas_call(
        _mod_kernel,
        grid=(depth, N_MOD * D_MODEL // tn),
        in_specs=[pl.BlockSpec((MOD_ROWS, D_MODEL), lambda l, j: (0, 0)),
                  pl.BlockSpec((1, D_MODEL, tn), lambda l, j: (l, 0, j)),
                  pl.BlockSpec((1, 1, tn), lambda l, j: (l, 0, j))],
        out_specs=pl.BlockSpec((1, MOD_ROWS, tn), lambda l, j: (l, 0, j)),
        out_shape=jax.ShapeDtypeStruct((depth, MOD_ROWS, N_MOD * D_MODEL), F32),
        compiler_params=_cparams(2),
        name="modulation",
    )(cvec, w_mod, b_mod.reshape(depth, 1, N_MOD * D_MODEL))


def _mod_spec(layer, mod_row):
    return pl.BlockSpec((None, None, N_MOD, D_MODEL), lambda b, *_: (layer, mod_row(b), 0, 0))


def _ffn_kernel(x_ref, mod_ref, g_ref, w1_ref, w3_ref, w2_ref, o_ref, *, sub):
    for r0 in range(0, x_ref.shape[1], sub):
        rows = slice(r0, r0 + sub)
        o_ref[0, rows, :] = _swiglu_rows(x_ref[0, rows, :], g_ref[0:1, :], mod_ref, 0,
                                         w1_ref, w3_ref, w2_ref)


def _ffn(x, mod_all, norm_g, w1, w3, w2, *, layer, mod_row, flatten, tm=FFN_ROWS):
    out_shape = x.shape
    if flatten:
        x = x.reshape(1, -1, D_MODEL)
    bsz, seq, _ = x.shape
    return pl.pallas_call(
        functools.partial(_ffn_kernel, sub=FFN_SUB_ROWS),
        grid=(bsz, seq // tm),
        in_specs=[pl.BlockSpec((1, tm, D_MODEL), lambda b, t: (b, t, 0)),
                  _mod_spec(layer, mod_row),
                  _const_spec((3, D_MODEL), (layer,)),
                  _const_spec((D_MODEL, D_FF), (layer, 0)),
                  _const_spec((D_MODEL, D_FF), (layer, 0)),
                  _const_spec((D_FF, D_MODEL), (layer, 0))],
        out_specs=pl.BlockSpec((1, tm, D_MODEL), lambda b, t: (b, t, 0)),
        out_shape=jax.ShapeDtypeStruct(x.shape, F32),
        compiler_params=_cparams(2),
        name="swiglu_half_step",
    )(x, mod_all, norm_g, w1, w3, w2).reshape(out_shape)


def _rope(t, cos, sin):
    return t * cos + pltpu.roll(t, HEAD_DIM // 2, 1) * sin


def _inproj_kernel(x_ref, mod_ref, g_ref, w_ref, *rest, rope, sub):
    if rope:
        cos_ref, sin_ref, pz_ref, q_ref, k_ref, v_ref, go_ref = rest
    else:
        pz_ref, q_ref, k_ref, v_ref, go_ref = rest
    k_scale = HEAD_DIM ** -0.5
    for r0 in range(0, x_ref.shape[1], sub):
        rows = slice(r0, r0 + sub)
        h = _rms_mod(x_ref[0, rows, :], g_ref[1:2, :], mod_ref[3:4, :], mod_ref[4:5, :]).astype(BF16)

        def proj(lo, hi):
            return jnp.dot(h, w_ref[:, lo:hi], preferred_element_type=F32)

        for off, dst, scale in ((Q_OFF, q_ref, None), (K_OFF, k_ref, k_scale)):
            t = proj(off, off + RET_WIDTH)
            for hd in range(RET_HEADS):
                lanes = slice(hd * HEAD_DIM, (hd + 1) * HEAD_DIM)
                th = t[:, lanes]
                if rope:
                    th = _rope(th, cos_ref[rows, :], sin_ref[rows, :])
                if scale is not None:
                    th = th * scale
                dst[0, rows, lanes] = th.astype(BF16)
        v_ref[0, rows, :] = proj(V_OFF, G_OFF).astype(BF16)
        go_ref[0, rows, :] = proj(G_OFF, C_OFF).astype(BF16)
        pz_ref[0, rows, :POOL_WIDTH] = proj(0, POOL_WIDTH)
        glu = proj(C_OFF, IN_WIDTH)
        pz_ref[0, rows, POOL_WIDTH:] = glu[:, :CONV_WIDTH] * _sigmoid(glu[:, CONV_WIDTH:])


def _inproj(x, mod_all, norm_g, w_in, rope_tabs, *, layer, mod_row, flatten, tm=INPROJ_ROWS):
    lead = x.shape[:2]
    if flatten:
        assert rope_tabs is None
        x = x.reshape(1, -1, D_MODEL)
    bsz, seq, _ = x.shape
    in_specs = [pl.BlockSpec((1, tm, D_MODEL), lambda b, t: (b, t, 0)),
                _mod_spec(layer, mod_row),
                _const_spec((3, D_MODEL), (layer,)),
                _const_spec((D_MODEL, IN_WIDTH), (layer,))]
    args = [x, mod_all, norm_g, w_in]
    if rope_tabs is not None:
        in_specs += [pl.BlockSpec((tm, HEAD_DIM), lambda b, t: (t, 0))] * 2
        args += list(rope_tabs)
    tok = lambda w: pl.BlockSpec((1, tm, w), lambda b, t: (b, t, 0))
    outs = pl.pallas_call(
        functools.partial(_inproj_kernel, rope=rope_tabs is not None, sub=INPROJ_SUB_ROWS),
        grid=(bsz, seq // tm),
        in_specs=in_specs,
        out_specs=[tok(PZ_WIDTH), tok(RET_WIDTH), tok(RET_WIDTH), tok(RET_WIDTH), tok(RET_WIDTH)],
        out_shape=[jax.ShapeDtypeStruct((bsz, seq, PZ_WIDTH), F32)]
        + [jax.ShapeDtypeStruct((bsz, seq, RET_WIDTH), BF16)] * 4,
        compiler_params=_cparams(2),
        name="mixer_in_proj",
    )(*args)
    return [o.reshape(lead + o.shape[2:]) for o in outs]


def _scan_kernel(kx_ref, vx_ref, ky_ref, vy_ref, wk_ref, cd_ref, sx_ref, sy_ref, kvx_ref, kvy_ref):
    n_x = kx_ref.shape[1] // CHUNK
    n_y = ky_ref.shape[1] // CHUNK

    def increments(k_ref, v_ref, kv_ref, n_chunks):
        def body(i, carry):
            for u in range(SCAN_UNROLL):
                n = i * SCAN_UNROLL + u
                rows = pl.ds(pl.multiple_of(n * CHUNK, CHUNK), CHUNK)
                for hd in range(RET_HEADS):
                    lanes = slice(hd * HEAD_DIM, (hd + 1) * HEAD_DIM)
                    v = v_ref[0, rows, lanes].astype(F32)
                    vw = (jnp.concatenate([v, v], axis=-1) * wk_ref[hd]).astype(BF16)
                    kv_ref[n, hd] = lax.dot_general(k_ref[0, rows, lanes], vw, (((0,), (0,)), ((), ())),
                                                    preferred_element_type=F32)
            return carry
        lax.fori_loop(0, n_chunks // SCAN_UNROLL, body, 0)

    increments(ky_ref, vy_ref, kvy_ref, n_y)
    increments(kx_ref, vx_ref, kvx_ref, n_x)

    for hd in range(RET_HEADS):
        for direction in (0, 1):
            lanes = slice(direction * HEAD_DIM, (direction + 1) * HEAD_DIM)
            decay = cd_ref[direction:direction + 1, hd * HEAD_DIM:(hd + 1) * HEAD_DIM]

            def scan(state, kv_ref, s_ref, n_chunks):
                def body(i, st):
                    n = i if direction == 0 else n_chunks - 1 - i
                    s_ref[0, n, hd, :, lanes] = st.astype(BF16)
                    return decay * st + kv_ref[n, hd, :, lanes]
                return lax.fori_loop(0, n_chunks, body, state)

            state = scan(jnp.zeros((HEAD_DIM, HEAD_DIM), F32), kvy_ref, sy_ref, n_y)
            scan(state, kvx_ref, sx_ref, n_x)


def _state_scan(kx, vx, ky, vy, wk, cd):
    bsz, seq, _ = kx.shape
    ctx = ky.shape[1]
    assert (seq // CHUNK) % SCAN_UNROLL == 0 and (ctx // CHUNK) % SCAN_UNROLL == 0
    tok = lambda n: pl.BlockSpec((1, n, RET_WIDTH), lambda b: (b, 0, 0))
    st = lambda n: pl.BlockSpec((1, n, RET_HEADS, HEAD_DIM, 2 * HEAD_DIM), lambda b: (b, 0, 0, 0, 0))
    return pl.pallas_call(
        _scan_kernel,
        grid=(bsz,),
        in_specs=[tok(seq), tok(seq), tok(ctx), tok(ctx),
                  _const_spec((RET_HEADS, CHUNK, 2 * HEAD_DIM)), _const_spec((2, RET_WIDTH))],
        out_specs=[st(seq // CHUNK), st(ctx // CHUNK)],
        out_shape=[jax.ShapeDtypeStruct((bsz, seq // CHUNK, RET_HEADS, HEAD_DIM, 2 * HEAD_DIM), BF16),
                   jax.ShapeDtypeStruct((bsz, ctx // CHUNK, RET_HEADS, HEAD_DIM, 2 * HEAD_DIM), BF16)],
        scratch_shapes=[pltpu.VMEM((seq // CHUNK, RET_HEADS, HEAD_DIM, 2 * HEAD_DIM), F32),
                        pltpu.VMEM((ctx // CHUNK, RET_HEADS, HEAD_DIM, 2 * HEAD_DIM), F32)],
        compiler_params=_cparams(1),
        name="retention_state_scan",
    )(kx, vx, ky, vy, wk, cd)


def _mixer_rows(c, t, tb, seq_len, x_ref, gate, q_ref, k_ref, v_ref, g_ref, s_ref, dmask_ref, wq_ref,
                poolw_ref, pscale_ref, gng_ref, dw_ref, db_ref, lng_ref, lnb_ref, wout_ref, ext_ref,
                sh_ref):
    r0 = c * CHUNK
    rows = slice(r0, r0 + CHUNK)

    def win(col, s):
        return ext_ref[HALO + r0 + s:HALO + r0 + s + CHUNK, col * LANES:(col + 1) * LANES]

    lane = lax.broadcasted_iota(jnp.int32, (CHUNK, LANES), 1)
    pos = lax.broadcasted_iota(jnp.int32, (CHUNK, LANES), 0) + (t * tb + r0)
    low_half = lane < POOL_GROUP
    pooled = []
    for col, (w_lo, w_hi) in enumerate(((2, 4), (8, 16))):
        acc = win(col, -w_lo // 2)
        for s in range(-w_lo // 2 + 1, w_lo // 2):
            acc = acc + win(col, s)
        wide = acc
        for s in list(range(-w_hi // 2, -w_lo // 2)) + list(range(w_lo // 2, w_hi // 2)):
            wide = wide + win(col, s)
        half = jnp.where(low_half, w_lo // 2, w_hi // 2)
        cnt = jnp.minimum(pos + half, seq_len) - jnp.maximum(pos - half, 0)
        pooled.append(jnp.where(low_half, acc, wide) / cnt.astype(F32) - win(col, 0))
    pooled = jnp.concatenate(pooled, axis=-1).astype(BF16)
    mixed = jnp.dot(pooled, poolw_ref[...], preferred_element_type=F32) * pscale_ref[...]
    pieces = [mixed.astype(BF16)]

    for hd in range(RET_HEADS):
        lanes = slice(hd * HEAD_DIM, (hd + 1) * HEAD_DIM)
        qh = q_ref[0, rows, lanes]
        kh = k_ref[0, rows, lanes]
        vh = v_ref[0, rows, lanes]
        sc = lax.dot_general(qh, kh, (((1,), (1,)), ((), ())), preferred_element_type=F32)
        prob = (sc * dmask_ref[hd]).astype(BF16)
        o = jnp.dot(prob, vh, preferred_element_type=F32)
        cross = jnp.dot(qh, s_ref[0, c, hd], preferred_element_type=F32) * wq_ref[hd]
        o = o + cross[:, :HEAD_DIM] + cross[:, HEAD_DIM:]
        mu = jnp.mean(o, axis=-1, keepdims=True)
        cen = o - mu
        var = jnp.mean(cen * cen, axis=-1, keepdims=True)
        y = cen * lax.rsqrt(var + EPS) * gng_ref[:, lanes]
        y = y * _silu(g_ref[0, rows, lanes].astype(F32))
        pieces.append(y.astype(BF16))

    rb = 64
    for b0 in range(r0, r0 + CHUNK, rb):
        acc = jnp.zeros((rb, CONV_WIDTH), F32) + db_ref[...]
        for kk in range(CONV_K):
            lo = b0 + kk + HALO - CONV_K // 2
            r = lo % SUBLANES
            if r == 0:
                src = ext_ref[lo:lo + rb, POOL_WIDTH:]
            else:
                src = sh_ref[r - 1, lo - r:lo - r + rb, :]
            acc = acc + dw_ref[kk:kk + 1, :] * src
        mu = jnp.mean(acc, axis=-1, keepdims=True)
        cen = acc - mu
        var = jnp.mean(cen * cen, axis=-1, keepdims=True)
        zn = cen * lax.rsqrt(var + EPS) * lng_ref[...] + lnb_ref[...]
        pieces.append(_silu(zn).astype(BF16))
    conv = jnp.concatenate(pieces[1 + RET_HEADS:], axis=0)
    cat = jnp.concatenate(pieces[:1 + RET_HEADS] + [conv], axis=-1)

    out = jnp.dot(cat, wout_ref[...], preferred_element_type=F32)
    return x_ref[0, rows, :] + gate * out


def _mix_ffn_kernel(x_ref, mod_ref, pz_ref, pzp_ref, pzn_ref, q_ref, k_ref, v_ref, g_ref, s_ref,
                    dmask_ref, wq_ref, poolw_ref, pscale_ref, gng_ref, dw_ref, db_ref, lng_ref,
                    lnb_ref, wout_ref, ng_ref, w1_ref, w3_ref, w2_ref, *rest, seq_len, tb, final):
    if final:
        fg_ref, o_ref, ext_ref, sh_ref = rest
    else:
        o_ref, ext_ref, sh_ref = rest
    t = pl.program_id(1)
    n_t = pl.num_programs(1)

    ext_ref[0:HALO, :] = jnp.where(t > 0, pzp_ref[0], 0.0)
    ext_ref[HALO:HALO + tb, :] = pz_ref[0]
    ext_ref[HALO + tb:, :] = jnp.where(t < n_t - 1, pzn_ref[0], 0.0)
    n_sh = sh_ref.shape[1]
    for r in range(1, SUBLANES):
        sh_ref[r - 1] = ext_ref[r:r + n_sh, POOL_WIDTH:]

    for c in range(tb // CHUNK):
        xm = _mixer_rows(c, t, tb, seq_len, x_ref, mod_ref[5:6, :], q_ref, k_ref, v_ref, g_ref, s_ref,
                         dmask_ref, wq_ref, poolw_ref, pscale_ref, gng_ref, dw_ref, db_ref, lng_ref,
                         lnb_ref, wout_ref, ext_ref, sh_ref)
        y = _swiglu_rows(xm, ng_ref[2:3, :], mod_ref, 6, w1_ref, w3_ref, w2_ref)
        if final:
            y = y * lax.rsqrt(jnp.mean(y * y, axis=-1, keepdims=True) + EPS) * fg_ref[...]
        o_ref[0, c * CHUNK:(c + 1) * CHUNK, :] = y


def _mixer_ffn(x, mod_all, pz, q, k, v, g, states, dmask, wq, poolw, pscale, gng, dw, db, lng, lnb,
               w_out, norm_g, w1, w3, w2, *, layer, mod_row, final_g=None, tb=MIX_ROWS):
    bsz, seq, _ = x.shape
    tb = min(tb, seq)
    n_halo = seq // HALO
    hb = tb // HALO
    tok = lambda w: pl.BlockSpec((1, tb, w), lambda b, t: (b, t, 0))
    in_specs = [
        tok(D_MODEL),
        _mod_spec(layer, mod_row),
        tok(PZ_WIDTH),
        pl.BlockSpec((1, HALO, PZ_WIDTH), lambda b, t: (b, jnp.maximum(t * hb - 1, 0), 0)),
        pl.BlockSpec((1, HALO, PZ_WIDTH), lambda b, t: (b, jnp.minimum((t + 1) * hb, n_halo - 1), 0)),
        tok(RET_WIDTH), tok(RET_WIDTH), tok(RET_WIDTH), tok(RET_WIDTH),
        pl.BlockSpec((1, tb // CHUNK, RET_HEADS, HEAD_DIM, 2 * HEAD_DIM), lambda b, t: (b, t, 0, 0, 0)),
        _const_spec((RET_HEADS, CHUNK, CHUNK)),
        _const_spec((RET_HEADS, CHUNK, 2 * HEAD_DIM)),
        _const_spec((POOL_WIDTH, POOL_WIDTH)),
        _const_spec((1, POOL_WIDTH)),
        _const_spec((1, RET_WIDTH)),
        _const_spec((CONV_K + 1, CONV_WIDTH)),
        _const_spec((1, CONV_WIDTH)),
        _const_spec((1, CONV_WIDTH)),
        _const_spec((1, CONV_WIDTH)),
        _const_spec((D_MODEL, D_MODEL), (layer,)),
        _const_spec((3, D_MODEL), (layer,)),
        _const_spec((D_MODEL, D_FF), (layer, 1)),
        _const_spec((D_MODEL, D_FF), (layer, 1)),
        _const_spec((D_FF, D_MODEL), (layer, 1)),
    ]
    args = [x, mod_all, pz, pz, pz, q, k, v, g, states, dmask, wq, poolw, pscale, gng, dw, db,
            lng, lnb, w_out, norm_g, w1, w3, w2]
    if final_g is not None:
        in_specs.append(_const_spec((1, D_MODEL)))
        args.append(final_g.reshape(1, D_MODEL))
    return pl.pallas_call(
        functools.partial(_mix_ffn_kernel, seq_len=seq, tb=tb, final=final_g is not None),
        grid=(bsz, seq // tb),
        in_specs=in_specs,
        out_specs=tok(D_MODEL),
        out_shape=jax.ShapeDtypeStruct(x.shape, F32),
        scratch_shapes=[pltpu.VMEM((tb + 2 * HALO, PZ_WIDTH), F32),
                        pltpu.VMEM((SUBLANES - 1, tb + 2 * HALO - SUBLANES, CONV_WIDTH), F32)],
        compiler_params=_cparams(2),
        name="mixer_swiglu",
    )(*args)


def _rope_tables(seq):
    n_freq = HEAD_DIM // 4
    inv = ROPE_BASE ** (-jnp.arange(n_freq, dtype=F32) / n_freq)
    rows = seq // GRID_W
    row = jnp.repeat(jnp.arange(rows, dtype=F32), GRID_W)
    col = jnp.tile(jnp.arange(GRID_W, dtype=F32), rows)
    ang_r = row[:, None] * inv[None]
    ang_c = col[:, None] * inv[None]
    cos = jnp.concatenate([jnp.cos(ang_r), jnp.cos(ang_c)] * 2, axis=-1)
    sin = jnp.concatenate([-jnp.sin(ang_r), -jnp.sin(ang_c), jnp.sin(ang_r), jnp.sin(ang_c)], axis=-1)
    return cos, sin


def _permute_heads(w_in):
    def perm(w):
        w = w.reshape(w.shape[:-1] + (RET_HEADS, 2, 2, HEAD_DIM // 4))
        return jnp.swapaxes(w, -2, -3).reshape(w.shape[:-4] + (RET_WIDTH,))
    return jnp.concatenate([w_in[..., :Q_OFF], perm(w_in[..., Q_OFF:K_OFF]),
                            perm(w_in[..., K_OFF:V_OFF]), w_in[..., V_OFF:]], axis=-1)


def _retention_tables(dec_f, dec_b):
    lg_f = jax.nn.log_sigmoid(dec_f.astype(F32))
    lg_b = jax.nn.log_sigmoid(dec_b.astype(F32))
    pos = jnp.arange(CHUNK, dtype=F32)
    diff = pos[:, None] - pos[None, :]
    dmask = jnp.where(diff[None] >= 0,
                      jnp.exp(jnp.maximum(diff, 0.0)[None] * lg_f[:, None, None]),
                      jnp.exp(jnp.maximum(-diff, 0.0)[None] * lg_b[:, None, None]))
    ones = jnp.ones((1, 1, HEAD_DIM), F32)
    wq_f = jnp.exp((pos + 1.0)[None, :] * lg_f[:, None])[:, :, None] * ones
    wq_b = jnp.exp((CHUNK - pos)[None, :] * lg_b[:, None])[:, :, None] * ones
    wq = jnp.concatenate([wq_f, wq_b], axis=-1)
    wk_f = jnp.exp((CHUNK - 1 - pos)[None, :] * lg_f[:, None])[:, :, None] * ones
    wk_b = jnp.exp(pos[None, :] * lg_b[:, None])[:, :, None] * ones
    wk = jnp.concatenate([wk_f, wk_b], axis=-1)
    cd = jnp.stack([jnp.repeat(jnp.exp(CHUNK * lg_f), HEAD_DIM),
                    jnp.repeat(jnp.exp(CHUNK * lg_b), HEAD_DIM)])
    return dmask, wq, wk, cd


def _block_diag(pool_w):
    out = jnp.zeros((POOL_WIDTH, POOL_WIDTH), pool_w.dtype)
    for gi in range(len(POOL_WINDOWS)):
        out = out.at[gi * POOL_GROUP:(gi + 1) * POOL_GROUP, gi * POOL_GROUP:(gi + 1) * POOL_GROUP].set(pool_w[gi])
    return out


def kernel(x, c, ctx, c_ctx, w_mod, b_mod, norm_g, ffn_w1, ffn_w3, ffn_w2, w_in, w_out, pool_w,
           pool_scale, ret_decay_fwd, ret_decay_bwd, ret_gn_g, conv_dw, conv_b, conv_ln_g,
           conv_ln_b, final_g):
    depth = w_mod.shape[0]
    bsz, seq, _ = x.shape
    cvec = jnp.zeros((MOD_ROWS, D_MODEL), F32).at[:bsz].set(c).at[bsz].set(c_ctx)
    mod_all = _modulation(cvec, w_mod, b_mod).reshape(depth, MOD_ROWS, N_MOD, D_MODEL)
    latent_row = lambda b: b
    context_row = lambda b: bsz
    rope_tabs = _rope_tables(seq)
    w1 = ffn_w1.astype(BF16)
    w3 = ffn_w3.astype(BF16)
    w2 = ffn_w2.astype(BF16)
    win = _permute_heads(w_in).astype(BF16)
    wout = w_out.astype(BF16)
    y = ctx
    for l in range(depth):
        last = l == depth - 1
        x = _ffn(x, mod_all, norm_g, w1, w3, w2, layer=l, mod_row=latent_row, flatten=False)
        y = _ffn(y, mod_all, norm_g, w1, w3, w2, layer=l, mod_row=context_row, flatten=True)

        dmask, wq, wk, cd = _retention_tables(ret_decay_fwd[l], ret_decay_bwd[l])
        pzx, qx, kx, vx, gx = _inproj(x, mod_all, norm_g, win, rope_tabs, layer=l, mod_row=latent_row,
                                      flatten=False)
        pzy, qy, ky, vy, gy = _inproj(y, mod_all, norm_g, win, None, layer=l, mod_row=context_row,
                                      flatten=True)
        sx, sy = _state_scan(kx, vx, ky, vy, wk, cd)
        params = (dmask, wq, _block_diag(pool_w[l]).astype(BF16), pool_scale[l].reshape(1, -1),
                  ret_gn_g[l].reshape(1, -1),
                  jnp.concatenate([conv_dw[l], jnp.zeros((1, CONV_WIDTH), F32)], axis=0),
                  conv_b[l].reshape(1, -1), conv_ln_g[l].reshape(1, -1), conv_ln_b[l].reshape(1, -1),
                  wout, norm_g, w1, w3, w2)
        x = _mixer_ffn(x, mod_all, pzx, qx, kx, vx, gx, sx, *params, layer=l, mod_row=latent_row,
                       final_g=final_g if last else None)
        if not last:
            y = _mixer_ffn(y, mod_all, pzy, qy, ky, vy, gy, sy, *params, layer=l, mod_row=context_row)
    return x
```

```python
import functools

import jax
import jax.numpy as jnp
from jax import lax
from jax.experimental import pallas as pl
from jax.experimental.pallas import tpu as pltpu

F32 = jnp.float32
BF16 = jnp.bfloat16

D_MODEL = 1024
N_MOD = 9
D_FF = 2816
POOL_WIDTH = 256
POOL_WINDOWS = (2, 4, 8, 16)
POOL_GROUP = 64
RET_WIDTH = 512
RET_HEADS = 4
HEAD_DIM = 128
CHUNK = 128
CONV_WIDTH = 256
CONV_K = 31
IN_WIDTH = 2816
Q_OFF = POOL_WIDTH
K_OFF = Q_OFF + RET_WIDTH
V_OFF = K_OFF + RET_WIDTH
G_OFF = V_OFF + RET_WIDTH
C_OFF = G_OFF + RET_WIDTH
GRID_W = 64
ROPE_BASE = 10000.0
EPS = 1e-6

SUBLANES = 8
LANES = 128
FFN_ROWS = 1024
FFN_SUB_ROWS = 128
INPROJ_ROWS = 1024
INPROJ_SUB_ROWS = 128
MIX_ROWS = 256
SCAN_UNROLL = 4
HALO = 16
PZ_WIDTH = POOL_WIDTH + CONV_WIDTH
MOD_ROWS = 24
VMEM_LIMIT = 56 * 1024 * 1024


def _cparams(n_axes):
    return pltpu.CompilerParams(dimension_semantics=("arbitrary",) * n_axes,
                                vmem_limit_bytes=VMEM_LIMIT)


def _const_spec(shape, lead=()):
    idx = tuple(lead) + (0,) * len(shape)
    return pl.BlockSpec((None,) * len(lead) + tuple(shape), lambda *_: idx,
                        pipeline_mode=pl.Buffered(1))


def _sigmoid(x):
    return 1.0 / (1.0 + jnp.exp(-x))


def _silu(x):
    return x * _sigmoid(x)


def _rms_mod(x, g, shift, scale):
    y = x * lax.rsqrt(jnp.mean(x * x, axis=-1, keepdims=True) + EPS) * g
    return y * (1.0 + scale) + shift


def _swiglu_rows(x, g, mod_ref, mod_base, w1_ref, w3_ref, w2_ref):
    shift = mod_ref[mod_base:mod_base + 1, :]
    scale = mod_ref[mod_base + 1:mod_base + 2, :]
    gate = mod_ref[mod_base + 2:mod_base + 3, :]
    h = _rms_mod(x, g, shift, scale).astype(BF16)
    a = jnp.dot(h, w1_ref[...], preferred_element_type=F32)
    b = jnp.dot(h, w3_ref[...], preferred_element_type=F32)
    u = (_silu(a) * b).astype(BF16)
    return x + 0.5 * gate * jnp.dot(u, w2_ref[...], preferred_element_type=F32)


def _mod_kernel(c_ref, w_ref, b_ref, o_ref):
    s = _silu(c_ref[...]).astype(BF16)
    o_ref[0] = jnp.dot(s, w_ref[0].astype(BF16), preferred_element_type=F32) + b_ref[0]


def _modulation(cvec, w_mod, b_mod):
    depth = w_mod.shape[0]
    tn = 1152
    return pl.pallas_call(
        _mod_kernel,
        grid=(depth, N_MOD * D_MODEL // tn),
        in_specs=[pl.BlockSpec((MOD_ROWS, D_MODEL), lambda l, j: (0, 0)),
                  pl.BlockSpec((1, D_MODEL, tn), lambda l, j: (l, 0, j)),
                  pl.BlockSpec((1, 1, tn), lambda l, j: (l, 0, j))],
        out_specs=pl.BlockSpec((1, MOD_ROWS, tn), lambda l, j: (l, 0, j)),
        out_shape=jax.ShapeDtypeStruct((depth, MOD_ROWS, N_MOD * D_MODEL), F32),
        compiler_params=_cparams(2),
        name="modulation",
    )(cvec, w_mod, b_mod.reshape(depth, 1, N_MOD * D_MODEL))


def _mod_spec(layer, mod_row):
    return pl.BlockSpec((None, None, N_MOD, D_MODEL), lambda b, *_: (layer, mod_row(b), 0, 0))


def _ffn_kernel(x_ref, mod_ref, g_ref, w1_ref, w3_ref, w2_ref, *rest, half, sub):
    fg_ref = rest[0] if len(rest) == 2 else None
    o_ref = rest[-1]
    for r0 in range(0, x_ref.shape[1], sub):
        rows = slice(r0, r0 + sub)
        y = _swiglu_rows(x_ref[0, rows, :], g_ref[2 * half:2 * half + 1, :], mod_ref, 6 * half,
                         w1_ref, w3_ref, w2_ref)
        if fg_ref is not None:
            y = y * lax.rsqrt(jnp.mean(y * y, axis=-1, keepdims=True) + EPS) * fg_ref[...]
        o_ref[0, rows, :] = y


def _ffn(x, mod_all, norm_g, w1, w3, w2, *, layer, half, mod_row, flatten, final_g=None, tm=FFN_ROWS):
    out_shape = x.shape
    if flatten:
        x = x.reshape(1, -1, D_MODEL)
    bsz, seq, _ = x.shape
    in_specs = [pl.BlockSpec((1, tm, D_MODEL), lambda b, t: (b, t, 0)),
                _mod_spec(layer, mod_row),
                _const_spec((3, D_MODEL), (layer,)),
                _const_spec((D_MODEL, D_FF), (layer, half)),
                _const_spec((D_MODEL, D_FF), (layer, half)),
                _const_spec((D_FF, D_MODEL), (layer, half))]
    args = [x, mod_all, norm_g, w1, w3, w2]
    if final_g is not None:
        in_specs.append(_const_spec((1, D_MODEL)))
        args.append(final_g.reshape(1, D_MODEL))
    return pl.pallas_call(
        functools.partial(_ffn_kernel, half=half, sub=FFN_SUB_ROWS),
        grid=(bsz, seq // tm),
        in_specs=in_specs,
        out_specs=pl.BlockSpec((1, tm, D_MODEL), lambda b, t: (b, t, 0)),
        out_shape=jax.ShapeDtypeStruct(x.shape, F32),
        compiler_params=_cparams(2),
        name="swiglu_half_step",
    )(*args).reshape(out_shape)


def _rope(t, cos, sin):
    return t * cos + pltpu.roll(t, HEAD_DIM // 2, 1) * sin


def _inproj_kernel(x_ref, mod_ref, g_ref, w_ref, *rest, rope, sub):
    if rope:
        cos_ref, sin_ref, pz_ref, q_ref, k_ref, v_ref, go_ref = rest
    else:
        pz_ref, q_ref, k_ref, v_ref, go_ref = rest
    k_scale = HEAD_DIM ** -0.5
    for r0 in range(0, x_ref.shape[1], sub):
        rows = slice(r0, r0 + sub)
        h = _rms_mod(x_ref[0, rows, :], g_ref[1:2, :], mod_ref[3:4, :], mod_ref[4:5, :]).astype(BF16)

        def proj(lo, hi):
            return jnp.dot(h, w_ref[:, lo:hi], preferred_element_type=F32)

        for off, dst, scale in ((Q_OFF, q_ref, None), (K_OFF, k_ref, k_scale)):
            t = proj(off, off + RET_WIDTH)
            for hd in range(RET_HEADS):
                lanes = slice(hd * HEAD_DIM, (hd + 1) * HEAD_DIM)
                th = t[:, lanes]
                if rope:
                    th = _rope(th, cos_ref[rows, :], sin_ref[rows, :])
                if scale is not None:
                    th = th * scale
                dst[0, rows, lanes] = th.astype(BF16)
        v_ref[0, rows, :] = proj(V_OFF, G_OFF).astype(BF16)
        go_ref[0, rows, :] = proj(G_OFF, C_OFF).astype(BF16)
        pz_ref[0, rows, :POOL_WIDTH] = proj(0, POOL_WIDTH)
        glu = proj(C_OFF, IN_WIDTH)
        pz_ref[0, rows, POOL_WIDTH:] = glu[:, :CONV_WIDTH] * _sigmoid(glu[:, CONV_WIDTH:])


def _inproj(x, mod_all, norm_g, w_in, rope_tabs, *, layer, mod_row, flatten, tm=INPROJ_ROWS):
    lead = x.shape[:2]
    if flatten:
        assert rope_tabs is None
        x = x.reshape(1, -1, D_MODEL)
    bsz, seq, _ = x.shape
    in_specs = [pl.BlockSpec((1, tm, D_MODEL), lambda b, t: (b, t, 0)),
                _mod_spec(layer, mod_row),
                _const_spec((3, D_MODEL), (layer,)),
                _const_spec((D_MODEL, IN_WIDTH), (layer,))]
    args = [x, mod_all, norm_g, w_in]
    if rope_tabs is not None:
        in_specs += [pl.BlockSpec((tm, HEAD_DIM), lambda b, t: (t, 0))] * 2
        args += list(rope_tabs)
    tok = lambda w: pl.BlockSpec((1, tm, w), lambda b, t: (b, t, 0))
    outs = pl.pallas_call(
        functools.partial(_inproj_kernel, rope=rope_tabs is not None, sub=INPROJ_SUB_ROWS),
        grid=(bsz, seq // tm),
        in_specs=in_specs,
        out_specs=[tok(PZ_WIDTH), tok(RET_WIDTH), tok(RET_WIDTH), tok(RET_WIDTH), tok(RET_WIDTH)],
        out_shape=[jax.ShapeDtypeStruct((bsz, seq, PZ_WIDTH), F32)]
        + [jax.ShapeDtypeStruct((bsz, seq, RET_WIDTH), BF16)] * 4,
        compiler_params=_cparams(2),
        name="mixer_in_proj",
    )(*args)
    return [o.reshape(lead + o.shape[2:]) for o in outs]


def _scan_kernel(kx_ref, vx_ref, ky_ref, vy_ref, wk_ref, cd_ref, sx_ref, sy_ref, kvx_ref, kvy_ref):
    n_x = kx_ref.shape[1] // CHUNK
    n_y = ky_ref.shape[1] // CHUNK

    def increments(k_ref, v_ref, kv_ref, n_chunks):
        unroll = min(SCAN_UNROLL, n_chunks)
        assert n_chunks % unroll == 0

        def body(i, carry):
            for u in range(unroll):
                n = i * unroll + u
                rows = pl.ds(pl.multiple_of(n * CHUNK, CHUNK), CHUNK)
                for hd in range(RET_HEADS):
                    lanes = slice(hd * HEAD_DIM, (hd + 1) * HEAD_DIM)
                    v = v_ref[0, rows, lanes].astype(F32)
                    vw = (jnp.concatenate([v, v], axis=-1) * wk_ref[hd]).astype(BF16)
                    kv_ref[n, hd] = lax.dot_general(k_ref[0, rows, lanes], vw, (((0,), (0,)), ((), ())),
                                                    preferred_element_type=F32)
            return carry
        lax.fori_loop(0, n_chunks // unroll, body, 0)

    increments(ky_ref, vy_ref, kvy_ref, n_y)
    increments(kx_ref, vx_ref, kvx_ref, n_x)

    for hd in range(RET_HEADS):
        for direction in (0, 1):
            lanes = slice(direction * HEAD_DIM, (direction + 1) * HEAD_DIM)
            decay = cd_ref[direction:direction + 1, hd * HEAD_DIM:(hd + 1) * HEAD_DIM]

            def scan(state, kv_ref, s_ref, n_chunks):
                def body(i, st):
                    n = i if direction == 0 else n_chunks - 1 - i
                    s_ref[0, n, hd, :, lanes] = st.astype(BF16)
                    return decay * st + kv_ref[n, hd, :, lanes]
                return lax.fori_loop(0, n_chunks, body, state)

            state = scan(jnp.zeros((HEAD_DIM, HEAD_DIM), F32), kvy_ref, sy_ref, n_y)
            scan(state, kvx_ref, sx_ref, n_x)


def _state_scan(kx, vx, ky, vy, wk, cd):
    bsz, seq, _ = kx.shape
    ctx = ky.shape[1]
    tok = lambda n: pl.BlockSpec((1, n, RET_WIDTH), lambda b: (b, 0, 0))
    st = lambda n: pl.BlockSpec((1, n, RET_HEADS, HEAD_DIM, 2 * HEAD_DIM), lambda b: (b, 0, 0, 0, 0))
    return pl.pallas_call(
        _scan_kernel,
        grid=(bsz,),
        in_specs=[tok(seq), tok(seq), tok(ctx), tok(ctx),
                  _const_spec((RET_HEADS, CHUNK, 2 * HEAD_DIM)), _const_spec((2, RET_WIDTH))],
        out_specs=[st(seq // CHUNK), st(ctx // CHUNK)],
        out_shape=[jax.ShapeDtypeStruct((bsz, seq // CHUNK, RET_HEADS, HEAD_DIM, 2 * HEAD_DIM), BF16),
                   jax.ShapeDtypeStruct((bsz, ctx // CHUNK, RET_HEADS, HEAD_DIM, 2 * HEAD_DIM), BF16)],
        scratch_shapes=[pltpu.VMEM((seq // CHUNK, RET_HEADS, HEAD_DIM, 2 * HEAD_DIM), F32),
                        pltpu.VMEM((ctx // CHUNK, RET_HEADS, HEAD_DIM, 2 * HEAD_DIM), F32)],
        compiler_params=_cparams(1),
        name="retention_state_scan",
    )(kx, vx, ky, vy, wk, cd)


def _mixer_rows(c, t, tb, seq_len, x_ref, gate, q_ref, k_ref, v_ref, g_ref, s_ref, dmask_ref, wq_ref,
                poolw_ref, pscale_ref, gng_ref, dw_ref, db_ref, lng_ref, lnb_ref, wout_ref, ext_ref,
                sh_ref, pool_ref):
    r0 = c * CHUNK
    rows = slice(r0, r0 + CHUNK)

    lane = lax.broadcasted_iota(jnp.int32, (CHUNK, LANES), 1)
    pos = lax.broadcasted_iota(jnp.int32, (CHUNK, LANES), 0) + (t * tb + r0)
    low_half = lane < POOL_GROUP
    pooled = []
    for col, (w_lo, w_hi) in enumerate(((2, 4), (8, 16))):
        lanes = slice(col * LANES, (col + 1) * LANES)
        half = jnp.where(low_half, w_lo // 2, w_hi // 2)
        cnt = jnp.minimum(pos + half, seq_len) - jnp.maximum(pos - half, 0)
        pooled.append(pool_ref[rows, lanes] / cnt.astype(F32)
                      - ext_ref[HALO + r0:HALO + r0 + CHUNK, lanes])
    pooled = jnp.concatenate(pooled, axis=-1).astype(BF16)
    mixed = jnp.dot(pooled, poolw_ref[...], preferred_element_type=F32) * pscale_ref[...]
    pieces = [mixed.astype(BF16)]

    for hd in range(RET_HEADS):
        lanes = slice(hd * HEAD_DIM, (hd + 1) * HEAD_DIM)
        qh = q_ref[0, rows, lanes]
        kh = k_ref[0, rows, lanes]
        vh = v_ref[0, rows, lanes]
        sc = lax.dot_general(qh, kh, (((1,), (1,)), ((), ())), preferred_element_type=F32)
        prob = (sc * dmask_ref[hd]).astype(BF16)
        o = jnp.dot(prob, vh, preferred_element_type=F32)
        cross = jnp.dot(qh, s_ref[0, c, hd], preferred_element_type=F32) * wq_ref[hd]
        o = o + cross[:, :HEAD_DIM] + cross[:, HEAD_DIM:]
        mu = jnp.mean(o, axis=-1, keepdims=True)
        cen = o - mu
        var = jnp.mean(cen * cen, axis=-1, keepdims=True)
        y = cen * lax.rsqrt(var + EPS) * gng_ref[:, lanes]
        y = y * _silu(g_ref[0, rows, lanes].astype(F32))
        pieces.append(y.astype(BF16))

    rb = 64
    for b0 in range(r0, r0 + CHUNK, rb):
        acc = jnp.zeros((rb, CONV_WIDTH), F32) + db_ref[...]
        for kk in range(CONV_K):
            lo = b0 + kk + HALO - CONV_K // 2
            r = lo % SUBLANES
            if r == 0:
                src = ext_ref[lo:lo + rb, POOL_WIDTH:]
            else:
                src = sh_ref[r - 1, lo - r:lo - r + rb, :]
            acc = acc + dw_ref[kk:kk + 1, :] * src
        mu = jnp.mean(acc, axis=-1, keepdims=True)
        cen = acc - mu
        var = jnp.mean(cen * cen, axis=-1, keepdims=True)
        zn = cen * lax.rsqrt(var + EPS) * lng_ref[...] + lnb_ref[...]
        pieces.append(_silu(zn).astype(BF16))
    conv = jnp.concatenate(pieces[1 + RET_HEADS:], axis=0)
    cat = jnp.concatenate(pieces[:1 + RET_HEADS] + [conv], axis=-1)

    out = jnp.dot(cat, wout_ref[...], preferred_element_type=F32)
    return x_ref[0, rows, :] + gate * out


def _mix_kernel(x_ref, mod_ref, pz_ref, pzp_ref, pzn_ref, q_ref, k_ref, v_ref, g_ref, s_ref,
                dmask_ref, wq_ref, poolw_ref, pscale_ref, gng_ref, dw_ref, db_ref, lng_ref,
                lnb_ref, wout_ref, o_ref, ext_ref, sh_ref, pool_ref, *, seq_len, tb):
    t = pl.program_id(1)
    n_t = pl.num_programs(1)

    ext_ref[0:HALO, :] = jnp.where(t > 0, pzp_ref[0], 0.0)
    ext_ref[HALO:HALO + tb, :] = pz_ref[0]
    ext_ref[HALO + tb:, :] = jnp.where(t < n_t - 1, pzn_ref[0], 0.0)
    n_sh = sh_ref.shape[1]
    for r in range(1, SUBLANES):
        sh_ref[r - 1] = ext_ref[r:r + n_sh, POOL_WIDTH:]

    n_ext = tb + 2 * HALO

    def pair(w, s):
        return pltpu.roll(w, s, 0) + pltpu.roll(w, n_ext - s, 0)

    low_half = lax.broadcasted_iota(jnp.int32, (n_ext, LANES), 1) < POOL_GROUP
    for col in range(POOL_WIDTH // LANES):
        lanes = slice(col * LANES, (col + 1) * LANES)
        e = ext_ref[:, lanes]
        w_lo = e + pltpu.roll(e, 1, 0)
        w_hi = pair(w_lo, 1)
        for _ in range(col):
            w_lo = pair(w_hi, 2)
            w_hi = pair(w_lo, 4)
        pool_ref[:, lanes] = jnp.where(low_half, w_lo, w_hi)[HALO:HALO + tb]

    for c in range(tb // CHUNK):
        o_ref[0, c * CHUNK:(c + 1) * CHUNK, :] = _mixer_rows(
            c, t, tb, seq_len, x_ref, mod_ref[5:6, :], q_ref, k_ref, v_ref, g_ref, s_ref, dmask_ref,
            wq_ref, poolw_ref, pscale_ref, gng_ref, dw_ref, db_ref, lng_ref, lnb_ref, wout_ref, ext_ref,
            sh_ref, pool_ref)


def _mixer_body(x, mod_all, pz, q, k, v, g, states, dmask, wq, poolw, pscale, gng, dw, db, lng, lnb,
                w_out, *, layer, mod_row, tb=MIX_ROWS):
    bsz, seq, _ = x.shape
    tb = min(tb, seq)
    n_halo = seq // HALO
    hb = tb // HALO
    tok = lambda w: pl.BlockSpec((1, tb, w), lambda b, t: (b, t, 0))
    in_specs = [
        tok(D_MODEL),
        _mod_spec(layer, mod_row),
        tok(PZ_WIDTH),
        pl.BlockSpec((1, HALO, PZ_WIDTH), lambda b, t: (b, jnp.maximum(t * hb - 1, 0), 0)),
        pl.BlockSpec((1, HALO, PZ_WIDTH), lambda b, t: (b, jnp.minimum((t + 1) * hb, n_halo - 1), 0)),
        tok(RET_WIDTH), tok(RET_WIDTH), tok(RET_WIDTH), tok(RET_WIDTH),
        pl.BlockSpec((1, tb // CHUNK, RET_HEADS, HEAD_DIM, 2 * HEAD_DIM), lambda b, t: (b, t, 0, 0, 0)),
        _const_spec((RET_HEADS, CHUNK, CHUNK)),
        _const_spec((RET_HEADS, CHUNK, 2 * HEAD_DIM)),
        _const_spec((POOL_WIDTH, POOL_WIDTH)),
        _const_spec((1, POOL_WIDTH)),
        _const_spec((1, RET_WIDTH)),
        _const_spec((CONV_K + 1, CONV_WIDTH)),
        _const_spec((1, CONV_WIDTH)),
        _const_spec((1, CONV_WIDTH)),
        _const_spec((1, CONV_WIDTH)),
        _const_spec((D_MODEL, D_MODEL), (layer,)),
    ]
    return pl.pallas_call(
        functools.partial(_mix_kernel, seq_len=seq, tb=tb),
        grid=(bsz, seq // tb),
        in_specs=in_specs,
        out_specs=tok(D_MODEL),
        out_shape=jax.ShapeDtypeStruct(x.shape, F32),
        scratch_shapes=[pltpu.VMEM((tb + 2 * HALO, PZ_WIDTH), F32),
                        pltpu.VMEM((SUBLANES - 1, tb + 2 * HALO - SUBLANES, CONV_WIDTH), F32),
                        pltpu.VMEM((tb, POOL_WIDTH), F32)],
        compiler_params=_cparams(2),
        name="mixer_body",
    )(x, mod_all, pz, pz, pz, q, k, v, g, states, dmask, wq, poolw, pscale, gng, dw, db, lng, lnb,
      w_out)


def _rope_tables(seq):
    n_freq = HEAD_DIM // 4
    inv = ROPE_BASE ** (-jnp.arange(n_freq, dtype=F32) / n_freq)
    rows = seq // GRID_W
    row = jnp.repeat(jnp.arange(rows, dtype=F32), GRID_W)
    col = jnp.tile(jnp.arange(GRID_W, dtype=F32), rows)
    ang_r = row[:, None] * inv[None]
    ang_c = col[:, None] * inv[None]
    cos = jnp.concatenate([jnp.cos(ang_r), jnp.cos(ang_c)] * 2, axis=-1)
    sin = jnp.concatenate([-jnp.sin(ang_r), -jnp.sin(ang_c), jnp.sin(ang_r), jnp.sin(ang_c)], axis=-1)
    return cos, sin


def _permute_heads(w_in):
    def perm(w):
        w = w.reshape(w.shape[:-1] + (RET_HEADS, 2, 2, HEAD_DIM // 4))
        return jnp.swapaxes(w, -2, -3).reshape(w.shape[:-4] + (RET_WIDTH,))
    return jnp.concatenate([w_in[..., :Q_OFF], perm(w_in[..., Q_OFF:K_OFF]),
                            perm(w_in[..., K_OFF:V_OFF]), w_in[..., V_OFF:]], axis=-1)


def _retention_tables(dec_f, dec_b):
    lg_f = jax.nn.log_sigmoid(dec_f.astype(F32))
    lg_b = jax.nn.log_sigmoid(dec_b.astype(F32))
    pos = jnp.arange(CHUNK, dtype=F32)
    diff = pos[:, None] - pos[None, :]
    dmask = jnp.where(diff[None] >= 0,
                      jnp.exp(jnp.maximum(diff, 0.0)[None] * lg_f[:, None, None]),
                      jnp.exp(jnp.maximum(-diff, 0.0)[None] * lg_b[:, None, None]))
    ones = jnp.ones((1, 1, HEAD_DIM), F32)
    wq_f = jnp.exp((pos + 1.0)[None, :] * lg_f[:, None])[:, :, None] * ones
    wq_b = jnp.exp((CHUNK - pos)[None, :] * lg_b[:, None])[:, :, None] * ones
    wq = jnp.concatenate([wq_f, wq_b], axis=-1)
    wk_f = jnp.exp((CHUNK - 1 - pos)[None, :] * lg_f[:, None])[:, :, None] * ones
    wk_b = jnp.exp(pos[None, :] * lg_b[:, None])[:, :, None] * ones
    wk = jnp.concatenate([wk_f, wk_b], axis=-1)
    cd = jnp.stack([jnp.repeat(jnp.exp(CHUNK * lg_f), HEAD_DIM),
                    jnp.repeat(jnp.exp(CHUNK * lg_b), HEAD_DIM)])
    return dmask, wq, wk, cd


def _block_diag(pool_w):
    out = jnp.zeros((POOL_WIDTH, POOL_WIDTH), pool_w.dtype)
    for gi in range(len(POOL_WINDOWS)):
        out = out.at[gi * POOL_GROUP:(gi + 1) * POOL_GROUP, gi * POOL_GROUP:(gi + 1) * POOL_GROUP].set(pool_w[gi])
    return out


def kernel(x, c, ctx, c_ctx, w_mod, b_mod, norm_g, ffn_w1, ffn_w3, ffn_w2, w_in, w_out, pool_w,
           pool_scale, ret_decay_fwd, ret_decay_bwd, ret_gn_g, conv_dw, conv_b, conv_ln_g,
           conv_ln_b, final_g):
    depth = w_mod.shape[0]
    bsz, seq, _ = x.shape
    cvec = jnp.zeros((MOD_ROWS, D_MODEL), F32).at[:bsz].set(c).at[bsz].set(c_ctx)
    mod_all = _modulation(cvec, w_mod, b_mod).reshape(depth, MOD_ROWS, N_MOD, D_MODEL)
    latent_row = lambda b: b
    context_row = lambda b: bsz
    rope_tabs = _rope_tables(seq)
    w1 = ffn_w1.astype(BF16)
    w3 = ffn_w3.astype(BF16)
    w2 = ffn_w2.astype(BF16)
    win = _permute_heads(w_in).astype(BF16)
    wout = w_out.astype(BF16)
    y = ctx
    for l in range(depth):
        last = l == depth - 1
        x = _ffn(x, mod_all, norm_g, w1, w3, w2, layer=l, half=0, mod_row=latent_row, flatten=False)
        y = _ffn(y, mod_all, norm_g, w1, w3, w2, layer=l, half=0, mod_row=context_row, flatten=True)

        dmask, wq, wk, cd = _retention_tables(ret_decay_fwd[l], ret_decay_bwd[l])
        pzx, qx, kx, vx, gx = _inproj(x, mod_all, norm_g, win, rope_tabs, layer=l, mod_row=latent_row,
                                      flatten=False)
        pzy, qy, ky, vy, gy = _inproj(y, mod_all, norm_g, win, None, layer=l, mod_row=context_row,
                                      flatten=True)
        sx, sy = _state_scan(kx, vx, ky, vy, wk, cd)
        params = (dmask, wq, _block_diag(pool_w[l]).astype(BF16), pool_scale[l].reshape(1, -1),
                  ret_gn_g[l].reshape(1, -1),
                  jnp.concatenate([conv_dw[l], jnp.zeros((1, CONV_WIDTH), F32)], axis=0),
                  conv_b[l].reshape(1, -1), conv_ln_g[l].reshape(1, -1), conv_ln_b[l].reshape(1, -1),
                  wout)
        x = _mixer_body(x, mod_all, pzx, qx, kx, vx, gx, sx, *params, layer=l, mod_row=latent_row)
        x = _ffn(x, mod_all, norm_g, w1, w3, w2, layer=l, half=1, mod_row=latent_row, flatten=False,
                 final_g=final_g if last else None)
        if not last:
            y = _mixer_body(y, mod_all, pzy, qy, ky, vy, gy, sy, *params, layer=l, mod_row=context_row)
            y = _ffn(y, mod_all, norm_g, w1, w3, w2, layer=l, half=1, mod_row=context_row, flatten=True)
    return x
```

```python
import functools

import jax
import jax.numpy as jnp
from jax import lax
from jax.experimental import pallas as pl
from jax.experimental.pallas import tpu as pltpu

F32 = jnp.float32
BF16 = jnp.bfloat16

D_MODEL = 1024
N_MOD = 9
D_FF = 2816
POOL_WIDTH = 256
POOL_WINDOWS = (2, 4, 8, 16)
POOL_GROUP = 64
RET_WIDTH = 512
RET_HEADS = 4
HEAD_DIM = 128
CHUNK = 128
CONV_WIDTH = 256
CONV_K = 31
IN_WIDTH = 2816
Q_OFF = POOL_WIDTH
K_OFF = Q_OFF + RET_WIDTH
V_OFF = K_OFF + RET_WIDTH
G_OFF = V_OFF + RET_WIDTH
C_OFF = G_OFF + RET_WIDTH
GRID_W = 64
ROPE_BASE = 10000.0
EPS = 1e-6

SUBLANES = 8
LANES = 128
FFN_ROWS = 512
FFN_SUB_ROWS = 128
INPROJ_ROWS = 1024
INPROJ_SUB_ROWS = 128
MIX_ROWS = 256
SCAN_UNROLL = 4
HALO = 16
PZ_WIDTH = POOL_WIDTH + CONV_WIDTH
MOD_ROWS = 24
VMEM_LIMIT = 56 * 1024 * 1024


def _cparams(n_axes):
    return pltpu.CompilerParams(dimension_semantics=("arbitrary",) * n_axes,
                                vmem_limit_bytes=VMEM_LIMIT)


def _const_spec(shape, lead=()):
    idx = tuple(lead) + (0,) * len(shape)
    return pl.BlockSpec((None,) * len(lead) + tuple(shape), lambda *_: idx,
                        pipeline_mode=pl.Buffered(1))


def _sigmoid(x):
    return 1.0 / (1.0 + jnp.exp(-x))


def _silu(x):
    return x * _sigmoid(x)


def _rms_mod(x, g, shift, scale):
    y = x * lax.rsqrt(jnp.mean(x * x, axis=-1, keepdims=True) + EPS) * g
    return y * (1.0 + scale) + shift


def _swiglu_rows(x, g, mod_ref, mod_base, w1_ref, w3_ref, w2_ref):
    shift = mod_ref[mod_base:mod_base + 1, :]
    scale = mod_ref[mod_base + 1:mod_base + 2, :]
    gate = mod_ref[mod_base + 2:mod_base + 3, :]
    h = _rms_mod(x, g, shift, scale).astype(BF16)
    a = jnp.dot(h, w1_ref[...], preferred_element_type=F32)
    b = jnp.dot(h, w3_ref[...], preferred_element_type=F32)
    u = (_silu(a) * b).astype(BF16)
    return x + 0.5 * gate * jnp.dot(u, w2_ref[...], preferred_element_type=F32)


def _mod_kernel(c_ref, w_ref, b_ref, o_ref):
    s = _silu(c_ref[...]).astype(BF16)
    o_ref[0] = jnp.dot(s, w_ref[0].astype(BF16), preferred_element_type=F32) + b_ref[0]


def _modulation(cvec, w_mod, b_mod):
    depth = w_mod.shape[0]
    tn = 1152
    return pl.pallas_call(
        _mod_kernel,
        grid=(depth, N_MOD * D_MODEL // tn),
        in_specs=[pl.BlockSpec((MOD_ROWS, D_MODEL), lambda l, j: (0, 0)),
                  pl.BlockSpec((1, D_MODEL, tn), lambda l, j: (l, 0, j)),
                  pl.BlockSpec((1, 1, tn), lambda l, j: (l, 0, j))],
        out_specs=pl.BlockSpec((1, MOD_ROWS, tn), lambda l, j: (l, 0, j)),
        out_shape=jax.ShapeDtypeStruct((depth, MOD_ROWS, N_MOD * D_MODEL), F32),
        compiler_params=_cparams(2),
        name="modulation",
    )(cvec, w_mod, b_mod.reshape(depth, 1, N_MOD * D_MODEL))


def _mod_spec(layer, mod_row):
    return pl.BlockSpec((None, None, N_MOD, D_MODEL), lambda b, *_: (layer, mod_row(b), 0, 0))


def _ffn_kernel(*refs, half, sub, n_x, has_ctx, final):
    refs = list(refs)
    x_ref = refs.pop(0)
    y_ref = refs.pop(0) if has_ctx else None
    mod_ref, g_ref, w1_ref, w3_ref, w2_ref = refs[:5]
    fg_ref = refs[5] if final else None
    ox_ref = refs[-2] if has_ctx else refs[-1]
    oy_ref = refs[-1] if has_ctx else None

    def run(src_ref, dst_ref, fg_ref):
        for r0 in range(0, src_ref.shape[1], sub):
            rows = slice(r0, r0 + sub)
            y = _swiglu_rows(src_ref[0, rows, :], g_ref[2 * half:2 * half + 1, :], mod_ref, 6 * half,
                             w1_ref, w3_ref, w2_ref)
            if fg_ref is not None:
                y = y * lax.rsqrt(jnp.mean(y * y, axis=-1, keepdims=True) + EPS) * fg_ref[...]
            dst_ref[0, rows, :] = y

    if has_ctx:
        i = pl.program_id(0)
        pl.when(i < n_x)(lambda: run(x_ref, ox_ref, fg_ref))
        pl.when(i >= n_x)(lambda: run(y_ref, oy_ref, None))
    else:
        run(x_ref, ox_ref, fg_ref)


def _ffn(x, y, mod_all, norm_g, w1, w3, w2, *, layer, half, final_g=None, tm=FFN_ROWS):
    bsz, seq, _ = x.shape
    n_t = seq // tm
    n_x = bsz * n_t
    has_ctx = y is not None
    x_idx = lambda i: jnp.minimum(i, n_x - 1)
    x_spec = pl.BlockSpec((1, tm, D_MODEL), lambda i: (x_idx(i) // n_t, x_idx(i) % n_t, 0))
    y_spec = pl.BlockSpec((1, tm, D_MODEL), lambda i: (0, jnp.maximum(i - n_x, 0), 0))
    mod_spec = pl.BlockSpec((None, None, N_MOD, D_MODEL),
                            lambda i: (layer, jnp.where(i < n_x, x_idx(i) // n_t, bsz), 0, 0))
    in_specs, args = [x_spec], [x]
    out_specs, out_shape = [x_spec], [jax.ShapeDtypeStruct(x.shape, F32)]
    n_y = 0
    if has_ctx:
        y_flat = y.reshape(1, -1, D_MODEL)
        n_y = y_flat.shape[1] // tm
        in_specs.append(y_spec)
        args.append(y_flat)
        out_specs.append(y_spec)
        out_shape.append(jax.ShapeDtypeStruct(y_flat.shape, F32))
    in_specs += [mod_spec,
                 _const_spec((3, D_MODEL), (layer,)),
                 _const_spec((D_MODEL, D_FF), (layer, half)),
                 _const_spec((D_MODEL, D_FF), (layer, half)),
                 _const_spec((D_FF, D_MODEL), (layer, half))]
    args += [mod_all, norm_g, w1, w3, w2]
    if final_g is not None:
        in_specs.append(_const_spec((1, D_MODEL)))
        args.append(final_g.reshape(1, D_MODEL))
    outs = pl.pallas_call(
        functools.partial(_ffn_kernel, half=half, sub=FFN_SUB_ROWS, n_x=n_x, has_ctx=has_ctx,
                          final=final_g is not None),
        grid=(n_x + n_y,),
        in_specs=in_specs,
        out_specs=out_specs,
        out_shape=out_shape,
        compiler_params=_cparams(1),
        name="swiglu_half_step",
    )(*args)
    return (outs[0], outs[1].reshape(y.shape)) if has_ctx else (outs[0], None)


def _rope(t, cos, sin):
    return t * cos + pltpu.roll(t, HEAD_DIM // 2, 1) * sin


def _inproj_kernel(x_ref, mod_ref, g_ref, w_ref, *rest, rope, kv_only, sub):
    if kv_only:
        k_ref, v_ref = rest
    elif rope:
        cos_ref, sin_ref, pz_ref, q_ref, k_ref, v_ref, go_ref = rest
    else:
        pz_ref, q_ref, k_ref, v_ref, go_ref = rest
    k_scale = HEAD_DIM ** -0.5
    for r0 in range(0, x_ref.shape[1], sub):
        rows = slice(r0, r0 + sub)
        h = _rms_mod(x_ref[0, rows, :], g_ref[1:2, :], mod_ref[3:4, :], mod_ref[4:5, :]).astype(BF16)

        def proj(lo, hi):
            return jnp.dot(h, w_ref[:, lo:hi], preferred_element_type=F32)

        heads = ((K_OFF, k_ref, k_scale),) if kv_only else ((Q_OFF, q_ref, None), (K_OFF, k_ref, k_scale))
        for off, dst, scale in heads:
            t = proj(off, off + RET_WIDTH)
            for hd in range(RET_HEADS):
                lanes = slice(hd * HEAD_DIM, (hd + 1) * HEAD_DIM)
                th = t[:, lanes]
                if rope:
                    th = _rope(th, cos_ref[rows, :], sin_ref[rows, :])
                if scale is not None:
                    th = th * scale
                dst[0, rows, lanes] = th.astype(BF16)
        v_ref[0, rows, :] = proj(V_OFF, G_OFF).astype(BF16)
        if kv_only:
            continue
        go_ref[0, rows, :] = proj(G_OFF, C_OFF).astype(BF16)
        pz_ref[0, rows, :POOL_WIDTH] = proj(0, POOL_WIDTH)
        glu = proj(C_OFF, IN_WIDTH)
        pz_ref[0, rows, POOL_WIDTH:] = glu[:, :CONV_WIDTH] * _sigmoid(glu[:, CONV_WIDTH:])


def _inproj(x, mod_all, norm_g, w_in, rope_tabs, *, layer, mod_row, flatten, kv_only=False,
            tm=INPROJ_ROWS):
    lead = x.shape[:2]
    if flatten:
        assert rope_tabs is None
        x = x.reshape(1, -1, D_MODEL)
    assert flatten or not kv_only
    bsz, seq, _ = x.shape
    in_specs = [pl.BlockSpec((1, tm, D_MODEL), lambda b, t: (b, t, 0)),
                _mod_spec(layer, mod_row),
                _const_spec((3, D_MODEL), (layer,)),
                _const_spec((D_MODEL, IN_WIDTH), (layer,))]
    args = [x, mod_all, norm_g, w_in]
    if rope_tabs is not None:
        in_specs += [pl.BlockSpec((tm, HEAD_DIM), lambda b, t: (t, 0))] * 2
        args += list(rope_tabs)
    tok = lambda w: pl.BlockSpec((1, tm, w), lambda b, t: (b, t, 0))
    n_ret = 2 if kv_only else 4
    out_specs = [tok(RET_WIDTH)] * n_ret
    out_shape = [jax.ShapeDtypeStruct((bsz, seq, RET_WIDTH), BF16)] * n_ret
    if not kv_only:
        out_specs = [tok(PZ_WIDTH)] + out_specs
        out_shape = [jax.ShapeDtypeStruct((bsz, seq, PZ_WIDTH), F32)] + out_shape
    outs = pl.pallas_call(
        functools.partial(_inproj_kernel, rope=rope_tabs is not None, kv_only=kv_only,
                          sub=INPROJ_SUB_ROWS),
        grid=(bsz, seq // tm),
        in_specs=in_specs,
        out_specs=out_specs,
        out_shape=out_shape,
        compiler_params=_cparams(2),
        name="mixer_in_proj",
    )(*args)
    return [o.reshape(lead + o.shape[2:]) for o in outs]


def _scan_kernel(kx_ref, vx_ref, ky_ref, vy_ref, wk_ref, cd_ref, sx_ref, sy_ref, kvx_ref, kvy_ref):
    n_x = kx_ref.shape[1] // CHUNK
    n_y = ky_ref.shape[1] // CHUNK

    def increments(k_ref, v_ref, kv_ref, n_chunks):
        unroll = min(SCAN_UNROLL, n_chunks)
        assert n_chunks % unroll == 0

        def body(i, carry):
            for u in range(unroll):
                n = i * unroll + u
                rows = pl.ds(pl.multiple_of(n * CHUNK, CHUNK), CHUNK)
                for hd in range(RET_HEADS):
                    lanes = slice(hd * HEAD_DIM, (hd + 1) * HEAD_DIM)
                    v = v_ref[0, rows, lanes].astype(F32)
                    vw = (jnp.concatenate([v, v], axis=-1) * wk_ref[hd]).astype(BF16)
                    kv_ref[n, hd] = lax.dot_general(k_ref[0, rows, lanes], vw, (((0,), (0,)), ((), ())),
                                                    preferred_element_type=F32)
            return carry
        lax.fori_loop(0, n_chunks // unroll, body, 0)

    increments(ky_ref, vy_ref, kvy_ref, n_y)
    increments(kx_ref, vx_ref, kvx_ref, n_x)

    for hd in range(RET_HEADS):
        for direction in (0, 1):
            lanes = slice(direction * HEAD_DIM, (direction + 1) * HEAD_DIM)
            decay = cd_ref[direction:direction + 1, hd * HEAD_DIM:(hd + 1) * HEAD_DIM]

            def scan(state, kv_ref, s_ref, n_chunks):
                def body(i, st):
                    n = i if direction == 0 else n_chunks - 1 - i
                    s_ref[0, n, hd, :, lanes] = st.astype(BF16)
                    return decay * st + kv_ref[n, hd, :, lanes]
                return lax.fori_loop(0, n_chunks, body, state)

            state = scan(jnp.zeros((HEAD_DIM, HEAD_DIM), F32), kvy_ref, sy_ref, n_y)
            scan(state, kvx_ref, sx_ref, n_x)


def _state_scan(kx, vx, ky, vy, wk, cd):
    bsz, seq, _ = kx.shape
    ctx = ky.shape[1]
    tok = lambda n: pl.BlockSpec((1, n, RET_WIDTH), lambda b: (b, 0, 0))
    st = lambda n: pl.BlockSpec((1, n, RET_HEADS, HEAD_DIM, 2 * HEAD_DIM), lambda b: (b, 0, 0, 0, 0))
    return pl.pallas_call(
        _scan_kernel,
        grid=(bsz,),
        in_specs=[tok(seq), tok(seq), tok(ctx), tok(ctx),
                  _const_spec((RET_HEADS, CHUNK, 2 * HEAD_DIM)), _const_spec((2, RET_WIDTH))],
        out_specs=[st(seq // CHUNK), st(ctx // CHUNK)],
        out_shape=[jax.ShapeDtypeStruct((bsz, seq // CHUNK, RET_HEADS, HEAD_DIM, 2 * HEAD_DIM), BF16),
                   jax.ShapeDtypeStruct((bsz, ctx // CHUNK, RET_HEADS, HEAD_DIM, 2 * HEAD_DIM), BF16)],
        scratch_shapes=[pltpu.VMEM((seq // CHUNK, RET_HEADS, HEAD_DIM, 2 * HEAD_DIM), F32),
                        pltpu.VMEM((ctx // CHUNK, RET_HEADS, HEAD_DIM, 2 * HEAD_DIM), F32)],
        compiler_params=_cparams(1),
        name="retention_state_scan",
    )(kx, vx, ky, vy, wk, cd)


def _mixer_rows(c, t, tb, seq_len, x_ref, gate, q_ref, k_ref, v_ref, g_ref, s_ref, dmask_ref, wq_ref,
                poolw_ref, pscale_ref, gng_ref, dw_ref, db_ref, lng_ref, lnb_ref, wout_ref, ext_ref,
                sh_ref, pool_ref):
    r0 = c * CHUNK
    rows = slice(r0, r0 + CHUNK)

    lane = lax.broadcasted_iota(jnp.int32, (CHUNK, LANES), 1)
    pos = lax.broadcasted_iota(jnp.int32, (CHUNK, LANES), 0) + (t * tb + r0)
    low_half = lane < POOL_GROUP
    pooled = []
    for col, (w_lo, w_hi) in enumerate(((2, 4), (8, 16))):
        lanes = slice(col * LANES, (col + 1) * LANES)
        half = jnp.where(low_half, w_lo // 2, w_hi // 2)
        cnt = jnp.minimum(pos + half, seq_len) - jnp.maximum(pos - half, 0)
        pooled.append(pool_ref[rows, lanes] / cnt.astype(F32)
                      - ext_ref[HALO + r0:HALO + r0 + CHUNK, lanes])
    pooled = jnp.concatenate(pooled, axis=-1).astype(BF16)
    mixed = jnp.dot(pooled, poolw_ref[...], preferred_element_type=F32) * pscale_ref[...]
    pieces = [mixed.astype(BF16)]

    for hd in range(RET_HEADS):
        lanes = slice(hd * HEAD_DIM, (hd + 1) * HEAD_DIM)
        qh = q_ref[0, rows, lanes]
        kh = k_ref[0, rows, lanes]
        vh = v_ref[0, rows, lanes]
        sc = lax.dot_general(qh, kh, (((1,), (1,)), ((), ())), preferred_element_type=F32)
        prob = (sc * dmask_ref[hd]).astype(BF16)
        o = jnp.dot(prob, vh, preferred_element_type=F32)
        cross = jnp.dot(qh, s_ref[0, c, hd], preferred_element_type=F32) * wq_ref[hd]
        o = o + cross[:, :HEAD_DIM] + cross[:, HEAD_DIM:]
        mu = jnp.mean(o, axis=-1, keepdims=True)
        cen = o - mu
        var = jnp.mean(cen * cen, axis=-1, keepdims=True)
        y = cen * lax.rsqrt(var + EPS) * gng_ref[:, lanes]
        y = y * _silu(g_ref[0, rows, lanes].astype(F32))
        pieces.append(y.astype(BF16))

    rb = 64
    for b0 in range(r0, r0 + CHUNK, rb):
        acc = jnp.zeros((rb, CONV_WIDTH), F32) + db_ref[...]
        for kk in range(CONV_K):
            lo = b0 + kk + HALO - CONV_K // 2
            r = lo % SUBLANES
            if r == 0:
                src = ext_ref[lo:lo + rb, POOL_WIDTH:]
            else:
                src = sh_ref[r - 1, lo - r:lo - r + rb, :]
            acc = acc + dw_ref[kk:kk + 1, :] * src
        mu = jnp.mean(acc, axis=-1, keepdims=True)
        cen = acc - mu
        var = jnp.mean(cen * cen, axis=-1, keepdims=True)
        zn = cen * lax.rsqrt(var + EPS) * lng_ref[...] + lnb_ref[...]
        pieces.append(_silu(zn).astype(BF16))
    conv = jnp.concatenate(pieces[1 + RET_HEADS:], axis=0)
    cat = jnp.concatenate(pieces[:1 + RET_HEADS] + [conv], axis=-1)

    out = jnp.dot(cat, wout_ref[...], preferred_element_type=F32)
    return x_ref[0, rows, :] + gate * out


def _mix_kernel(x_ref, mod_ref, pz_ref, pzp_ref, pzn_ref, q_ref, k_ref, v_ref, g_ref, s_ref,
                dmask_ref, wq_ref, poolw_ref, pscale_ref, gng_ref, dw_ref, db_ref, lng_ref,
                lnb_ref, wout_ref, o_ref, ext_ref, sh_ref, pool_ref, *, seq_len, tb):
    t = pl.program_id(1)
    n_t = pl.num_programs(1)

    ext_ref[0:HALO, :] = jnp.where(t > 0, pzp_ref[0], 0.0)
    ext_ref[HALO:HALO + tb, :] = pz_ref[0]
    ext_ref[HALO + tb:, :] = jnp.where(t < n_t - 1, pzn_ref[0], 0.0)
    n_sh = sh_ref.shape[1]
    for r in range(1, SUBLANES):
        sh_ref[r - 1] = ext_ref[r:r + n_sh, POOL_WIDTH:]

    n_ext = tb + 2 * HALO

    def pair(w, s):
        return pltpu.roll(w, s, 0) + pltpu.roll(w, n_ext - s, 0)

    low_half = lax.broadcasted_iota(jnp.int32, (n_ext, LANES), 1) < POOL_GROUP
    for col in range(POOL_WIDTH // LANES):
        lanes = slice(col * LANES, (col + 1) * LANES)
        e = ext_ref[:, lanes]
        w_lo = e + pltpu.roll(e, 1, 0)
        w_hi = pair(w_lo, 1)
        for _ in range(col):
            w_lo = pair(w_hi, 2)
            w_hi = pair(w_lo, 4)
        pool_ref[:, lanes] = jnp.where(low_half, w_lo, w_hi)[HALO:HALO + tb]

    for c in range(tb // CHUNK):
        o_ref[0, c * CHUNK:(c + 1) * CHUNK, :] = _mixer_rows(
            c, t, tb, seq_len, x_ref, mod_ref[5:6, :], q_ref, k_ref, v_ref, g_ref, s_ref, dmask_ref,
            wq_ref, poolw_ref, pscale_ref, gng_ref, dw_ref, db_ref, lng_ref, lnb_ref, wout_ref, ext_ref,
            sh_ref, pool_ref)


def _mixer_body(x, mod_all, pz, q, k, v, g, states, dmask, wq, poolw, pscale, gng, dw, db, lng, lnb,
                w_out, *, layer, mod_row, tb=MIX_ROWS):
    bsz, seq, _ = x.shape
    tb = min(tb, seq)
    n_halo = seq // HALO
    hb = tb // HALO
    tok = lambda w: pl.BlockSpec((1, tb, w), lambda b, t: (b, t, 0))
    in_specs = [
        tok(D_MODEL),
        _mod_spec(layer, mod_row),
        tok(PZ_WIDTH),
        pl.BlockSpec((1, HALO, PZ_WIDTH), lambda b, t: (b, jnp.maximum(t * hb - 1, 0), 0)),
        pl.BlockSpec((1, HALO, PZ_WIDTH), lambda b, t: (b, jnp.minimum((t + 1) * hb, n_halo - 1), 0)),
        tok(RET_WIDTH), tok(RET_WIDTH), tok(RET_WIDTH), tok(RET_WIDTH),
        pl.BlockSpec((1, tb // CHUNK, RET_HEADS, HEAD_DIM, 2 * HEAD_DIM), lambda b, t: (b, t, 0, 0, 0)),
        _const_spec((RET_HEADS, CHUNK, CHUNK)),
        _const_spec((RET_HEADS, CHUNK, 2 * HEAD_DIM)),
        _const_spec((POOL_WIDTH, POOL_WIDTH)),
        _const_spec((1, POOL_WIDTH)),
        _const_spec((1, RET_WIDTH)),
        _const_spec((CONV_K + 1, CONV_WIDTH)),
        _const_spec((1, CONV_WIDTH)),
        _const_spec((1, CONV_WIDTH)),
        _const_spec((1, CONV_WIDTH)),
        _const_spec((D_MODEL, D_MODEL), (layer,)),
    ]
    return pl.pallas_call(
        functools.partial(_mix_kernel, seq_len=seq, tb=tb),
        grid=(bsz, seq // tb),
        in_specs=in_specs,
        out_specs=tok(D_MODEL),
        out_shape=jax.ShapeDtypeStruct(x.shape, F32),
        scratch_shapes=[pltpu.VMEM((tb + 2 * HALO, PZ_WIDTH), F32),
                        pltpu.VMEM((SUBLANES - 1, tb + 2 * HALO - SUBLANES, CONV_WIDTH), F32),
                        pltpu.VMEM((tb, POOL_WIDTH), F32)],
        compiler_params=_cparams(2),
        name="mixer_body",
    )(x, mod_all, pz, pz, pz, q, k, v, g, states, dmask, wq, poolw, pscale, gng, dw, db, lng, lnb,
      w_out)


def _rope_tables(seq):
    n_freq = HEAD_DIM // 4
    inv = ROPE_BASE ** (-jnp.arange(n_freq, dtype=F32) / n_freq)
    rows = seq // GRID_W
    row = jnp.repeat(jnp.arange(rows, dtype=F32), GRID_W)
    col = jnp.tile(jnp.arange(GRID_W, dtype=F32), rows)
    ang_r = row[:, None] * inv[None]
    ang_c = col[:, None] * inv[None]
    cos = jnp.concatenate([jnp.cos(ang_r), jnp.cos(ang_c)] * 2, axis=-1)
    sin = jnp.concatenate([-jnp.sin(ang_r), -jnp.sin(ang_c), jnp.sin(ang_r), jnp.sin(ang_c)], axis=-1)
    return cos, sin


def _permute_heads(w_in):
    def perm(w):
        w = w.reshape(w.shape[:-1] + (RET_HEADS, 2, 2, HEAD_DIM // 4))
        return jnp.swapaxes(w, -2, -3).reshape(w.shape[:-4] + (RET_WIDTH,))
    return jnp.concatenate([w_in[..., :Q_OFF], perm(w_in[..., Q_OFF:K_OFF]),
                            perm(w_in[..., K_OFF:V_OFF]), w_in[..., V_OFF:]], axis=-1)


def _retention_tables(dec_f, dec_b):
    lg_f = jax.nn.log_sigmoid(dec_f.astype(F32))
    lg_b = jax.nn.log_sigmoid(dec_b.astype(F32))
    pos = jnp.arange(CHUNK, dtype=F32)
    diff = pos[:, None] - pos[None, :]
    dmask = jnp.where(diff[None] >= 0,
                      jnp.exp(jnp.maximum(diff, 0.0)[None] * lg_f[:, None, None]),
                      jnp.exp(jnp.maximum(-diff, 0.0)[None] * lg_b[:, None, None]))
    ones = jnp.ones((1, 1, HEAD_DIM), F32)
    wq_f = jnp.exp((pos + 1.0)[None, :] * lg_f[:, None])[:, :, None] * ones
    wq_b = jnp.exp((CHUNK - pos)[None, :] * lg_b[:, None])[:, :, None] * ones
    wq = jnp.concatenate([wq_f, wq_b], axis=-1)
    wk_f = jnp.exp((CHUNK - 1 - pos)[None, :] * lg_f[:, None])[:, :, None] * ones
    wk_b = jnp.exp(pos[None, :] * lg_b[:, None])[:, :, None] * ones
    wk = jnp.concatenate([wk_f, wk_b], axis=-1)
    cd = jnp.stack([jnp.repeat(jnp.exp(CHUNK * lg_f), HEAD_DIM),
                    jnp.repeat(jnp.exp(CHUNK * lg_b), HEAD_DIM)])
    return dmask, wq, wk, cd


def _block_diag(pool_w):
    out = jnp.zeros((POOL_WIDTH, POOL_WIDTH), pool_w.dtype)
    for gi in range(len(POOL_WINDOWS)):
        out = out.at[gi * POOL_GROUP:(gi + 1) * POOL_GROUP, gi * POOL_GROUP:(gi + 1) * POOL_GROUP].set(pool_w[gi])
    return out


def kernel(x, c, ctx, c_ctx, w_mod, b_mod, norm_g, ffn_w1, ffn_w3, ffn_w2, w_in, w_out, pool_w,
           pool_scale, ret_decay_fwd, ret_decay_bwd, ret_gn_g, conv_dw, conv_b, conv_ln_g,
           conv_ln_b, final_g):
    depth = w_mod.shape[0]
    bsz, seq, _ = x.shape
    cvec = jnp.zeros((MOD_ROWS, D_MODEL), F32).at[:bsz].set(c).at[bsz].set(c_ctx)
    mod_all = _modulation(cvec, w_mod, b_mod).reshape(depth, MOD_ROWS, N_MOD, D_MODEL)
    latent_row = lambda b: b
    context_row = lambda b: bsz
    rope_tabs = _rope_tables(seq)
    w1 = ffn_w1.astype(BF16)
    w3 = ffn_w3.astype(BF16)
    w2 = ffn_w2.astype(BF16)
    win = _permute_heads(w_in).astype(BF16)
    wout = w_out.astype(BF16)
    y = ctx
    for l in range(depth):
        last = l == depth - 1
        x, y = _ffn(x, y, mod_all, norm_g, w1, w3, w2, layer=l, half=0)

        dmask, wq, wk, cd = _retention_tables(ret_decay_fwd[l], ret_decay_bwd[l])
        pzx, qx, kx, vx, gx = _inproj(x, mod_all, norm_g, win, rope_tabs, layer=l, mod_row=latent_row,
                                      flatten=False)
        if last:
            ky, vy = _inproj(y, mod_all, norm_g, win, None, layer=l, mod_row=context_row,
                             flatten=True, kv_only=True)
        else:
            pzy, qy, ky, vy, gy = _inproj(y, mod_all, norm_g, win, None, layer=l, mod_row=context_row,
                                          flatten=True)
        sx, sy = _state_scan(kx, vx, ky, vy, wk, cd)
        params = (dmask, wq, _block_diag(pool_w[l]).astype(BF16), pool_scale[l].reshape(1, -1),
                  ret_gn_g[l].reshape(1, -1),
                  jnp.concatenate([conv_dw[l], jnp.zeros((1, CONV_WIDTH), F32)], axis=0),
                  conv_b[l].reshape(1, -1), conv_ln_g[l].reshape(1, -1), conv_ln_b[l].reshape(1, -1),
                  wout)
        x = _mixer_body(x, mod_all, pzx, qx, kx, vx, gx, sx, *params, layer=l, mod_row=latent_row)
        if last:
            x, _ = _ffn(x, None, mod_all, norm_g, w1, w3, w2, layer=l, half=1, final_g=final_g)
        else:
            y = _mixer_body(y, mod_all, pzy, qy, ky, vy, gy, sy, *params, layer=l, mod_row=context_row)
            x, y = _ffn(x, y, mod_all, norm_g, w1, w3, w2, layer=l, half=1)
    return x
```

```python
import functools

import jax
import jax.numpy as jnp
from jax import lax
from jax.experimental import pallas as pl
from jax.experimental.pallas import tpu as pltpu

F32 = jnp.float32
BF16 = jnp.bfloat16

D_MODEL = 1024
N_MOD = 9
D_FF = 2816
POOL_WIDTH = 256
POOL_WINDOWS = (2, 4, 8, 16)
POOL_GROUP = 64
RET_WIDTH = 512
RET_HEADS = 4
HEAD_DIM = 128
CHUNK = 128
CONV_WIDTH = 256
CONV_K = 31
IN_WIDTH = 2816
Q_OFF = POOL_WIDTH
K_OFF = Q_OFF + RET_WIDTH
V_OFF = K_OFF + RET_WIDTH
G_OFF = V_OFF + RET_WIDTH
C_OFF = G_OFF + RET_WIDTH
GRID_W = 64
ROPE_BASE = 10000.0
EPS = 1e-6

SUBLANES = 8
LANES = 128
FFN_ROWS = 512
FFN_SUB_ROWS = 128
INPROJ_ROWS = 1024
INPROJ_SUB_ROWS = 128
MIX_ROWS = 512
SCAN_UNROLL = 4
HALO = 16
PZ_WIDTH = POOL_WIDTH + CONV_WIDTH
MOD_ROWS = 24
VMEM_LIMIT = 56 * 1024 * 1024


def _cparams(n_axes):
    return pltpu.CompilerParams(dimension_semantics=("arbitrary",) * n_axes,
                                vmem_limit_bytes=VMEM_LIMIT)


def _const_spec(shape, lead=()):
    idx = tuple(lead) + (0,) * len(shape)
    return pl.BlockSpec((None,) * len(lead) + tuple(shape), lambda *_: idx,
                        pipeline_mode=pl.Buffered(1))


def _sigmoid(x):
    return 1.0 / (1.0 + jnp.exp(-x))


def _silu(x):
    return x * _sigmoid(x)


def _rms_mod(x, g, shift, scale):
    y = x * lax.rsqrt(jnp.mean(x * x, axis=-1, keepdims=True) + EPS) * g
    return y * (1.0 + scale) + shift


def _swiglu_rows(x, g, mod_ref, mod_base, w1_ref, w3_ref, w2_ref):
    shift = mod_ref[mod_base:mod_base + 1, :]
    scale = mod_ref[mod_base + 1:mod_base + 2, :]
    gate = mod_ref[mod_base + 2:mod_base + 3, :]
    h = _rms_mod(x, g, shift, scale).astype(BF16)
    a = jnp.dot(h, w1_ref[...], preferred_element_type=F32)
    b = jnp.dot(h, w3_ref[...], preferred_element_type=F32)
    u = (_silu(a) * b).astype(BF16)
    return x + 0.5 * gate * jnp.dot(u, w2_ref[...], preferred_element_type=F32)


def _mod_kernel(c_ref, w_ref, b_ref, o_ref):
    s = _silu(c_ref[...]).astype(BF16)
    o_ref[0] = jnp.dot(s, w_ref[0].astype(BF16), preferred_element_type=F32) + b_ref[0]


def _modulation(cvec, w_mod, b_mod):
    depth = w_mod.shape[0]
    tn = 1152
    return pl.pallas_call(
        _mod_kernel,
        grid=(depth, N_MOD * D_MODEL // tn),
        in_specs=[pl.BlockSpec((MOD_ROWS, D_MODEL), lambda l, j: (0, 0)),
                  pl.BlockSpec((1, D_MODEL, tn), lambda l, j: (l, 0, j)),
                  pl.BlockSpec((1, 1, tn), lambda l, j: (l, 0, j))],
        out_specs=pl.BlockSpec((1, MOD_ROWS, tn), lambda l, j: (l, 0, j)),
        out_shape=jax.ShapeDtypeStruct((depth, MOD_ROWS, N_MOD * D_MODEL), F32),
        compiler_params=_cparams(2),
        name="modulation",
    )(cvec, w_mod, b_mod.reshape(depth, 1, N_MOD * D_MODEL))


def _mod_spec(layer, mod_row):
    return pl.BlockSpec((None, None, N_MOD, D_MODEL), lambda b, *_: (layer, mod_row(b), 0, 0))


def _ffn_kernel(*refs, half, sub, n_x, has_ctx, final):
    refs = list(refs)
    x_ref = refs.pop(0)
    y_ref = refs.pop(0) if has_ctx else None
    mod_ref, g_ref, w1_ref, w3_ref, w2_ref = refs[:5]
    fg_ref = refs[5] if final else None
    ox_ref = refs[-2] if has_ctx else refs[-1]
    oy_ref = refs[-1] if has_ctx else None

    def run(src_ref, dst_ref, fg_ref):
        for r0 in range(0, src_ref.shape[1], sub):
            rows = slice(r0, r0 + sub)
            y = _swiglu_rows(src_ref[0, rows, :], g_ref[2 * half:2 * half + 1, :], mod_ref, 6 * half,
                             w1_ref, w3_ref, w2_ref)
            if fg_ref is not None:
                y = y * lax.rsqrt(jnp.mean(y * y, axis=-1, keepdims=True) + EPS) * fg_ref[...]
            dst_ref[0, rows, :] = y

    if has_ctx:
        i = pl.program_id(0)
        pl.when(i < n_x)(lambda: run(x_ref, ox_ref, fg_ref))
        pl.when(i >= n_x)(lambda: run(y_ref, oy_ref, None))
    else:
        run(x_ref, ox_ref, fg_ref)


def _ffn(x, y, mod_all, norm_g, w1, w3, w2, *, layer, half, final_g=None, tm=FFN_ROWS):
    bsz, seq, _ = x.shape
    n_t = seq // tm
    n_x = bsz * n_t
    has_ctx = y is not None
    x_idx = lambda i: jnp.minimum(i, n_x - 1)
    x_spec = pl.BlockSpec((1, tm, D_MODEL), lambda i: (x_idx(i) // n_t, x_idx(i) % n_t, 0))
    y_spec = pl.BlockSpec((1, tm, D_MODEL), lambda i: (0, jnp.maximum(i - n_x, 0), 0))
    mod_spec = pl.BlockSpec((None, None, N_MOD, D_MODEL),
                            lambda i: (layer, jnp.where(i < n_x, x_idx(i) // n_t, bsz), 0, 0))
    in_specs, args = [x_spec], [x]
    out_specs, out_shape = [x_spec], [jax.ShapeDtypeStruct(x.shape, F32)]
    n_y = 0
    if has_ctx:
        y_flat = y.reshape(1, -1, D_MODEL)
        n_y = y_flat.shape[1] // tm
        in_specs.append(y_spec)
        args.append(y_flat)
        out_specs.append(y_spec)
        out_shape.append(jax.ShapeDtypeStruct(y_flat.shape, F32))
    in_specs += [mod_spec,
                 _const_spec((3, D_MODEL), (layer,)),
                 _const_spec((D_MODEL, D_FF), (layer, half)),
                 _const_spec((D_MODEL, D_FF), (layer, half)),
                 _const_spec((D_FF, D_MODEL), (layer, half))]
    args += [mod_all, norm_g, w1, w3, w2]
    if final_g is not None:
        in_specs.append(_const_spec((1, D_MODEL)))
        args.append(final_g.reshape(1, D_MODEL))
    outs = pl.pallas_call(
        functools.partial(_ffn_kernel, half=half, sub=FFN_SUB_ROWS, n_x=n_x, has_ctx=has_ctx,
                          final=final_g is not None),
        grid=(n_x + n_y,),
        in_specs=in_specs,
        out_specs=out_specs,
        out_shape=out_shape,
        compiler_params=_cparams(1),
        name="swiglu_half_step",
    )(*args)
    return (outs[0], outs[1].reshape(y.shape)) if has_ctx else (outs[0], None)


def _rope(t, cos, sin):
    return t * cos + pltpu.roll(t, HEAD_DIM // 2, 1) * sin


def _inproj_kernel(x_ref, mod_ref, g_ref, w_ref, *rest, rope, kv_only, sub):
    if kv_only:
        k_ref, v_ref = rest
    elif rope:
        cos_ref, sin_ref, pz_ref, q_ref, k_ref, v_ref, go_ref = rest
    else:
        pz_ref, q_ref, k_ref, v_ref, go_ref = rest
    k_scale = HEAD_DIM ** -0.5
    for r0 in range(0, x_ref.shape[1], sub):
        rows = slice(r0, r0 + sub)
        h = _rms_mod(x_ref[0, rows, :], g_ref[1:2, :], mod_ref[3:4, :], mod_ref[4:5, :]).astype(BF16)

        def proj(lo, hi):
            return jnp.dot(h, w_ref[:, lo:hi], preferred_element_type=F32)

        heads = ((K_OFF, k_ref, k_scale),) if kv_only else ((Q_OFF, q_ref, None), (K_OFF, k_ref, k_scale))
        for off, dst, scale in heads:
            t = proj(off, off + RET_WIDTH)
            for hd in range(RET_HEADS):
                lanes = slice(hd * HEAD_DIM, (hd + 1) * HEAD_DIM)
                th = t[:, lanes]
                if rope:
                    th = _rope(th, cos_ref[rows, :], sin_ref[rows, :])
                if scale is not None:
                    th = th * scale
                dst[0, rows, lanes] = th.astype(BF16)
        v_ref[0, rows, :] = proj(V_OFF, G_OFF).astype(BF16)
        if kv_only:
            continue
        go_ref[0, rows, :] = proj(G_OFF, C_OFF).astype(BF16)
        pz_ref[0, rows, :POOL_WIDTH] = proj(0, POOL_WIDTH)
        glu = proj(C_OFF, IN_WIDTH)
        pz_ref[0, rows, POOL_WIDTH:] = glu[:, :CONV_WIDTH] * _sigmoid(glu[:, CONV_WIDTH:])


def _inproj(x, mod_all, norm_g, w_in, rope_tabs, *, layer, mod_row, flatten, kv_only=False,
            tm=INPROJ_ROWS):
    lead = x.shape[:2]
    if flatten:
        assert rope_tabs is None
        x = x.reshape(1, -1, D_MODEL)
    assert flatten or not kv_only
    bsz, seq, _ = x.shape
    in_specs = [pl.BlockSpec((1, tm, D_MODEL), lambda b, t: (b, t, 0)),
                _mod_spec(layer, mod_row),
                _const_spec((3, D_MODEL), (layer,)),
                _const_spec((D_MODEL, IN_WIDTH), (layer,))]
    args = [x, mod_all, norm_g, w_in]
    if rope_tabs is not None:
        in_specs += [pl.BlockSpec((tm, HEAD_DIM), lambda b, t: (t, 0))] * 2
        args += list(rope_tabs)
    tok = lambda w: pl.BlockSpec((1, tm, w), lambda b, t: (b, t, 0))
    n_ret = 2 if kv_only else 4
    out_specs = [tok(RET_WIDTH)] * n_ret
    out_shape = [jax.ShapeDtypeStruct((bsz, seq, RET_WIDTH), BF16)] * n_ret
    if not kv_only:
        out_specs = [tok(PZ_WIDTH)] + out_specs
        out_shape = [jax.ShapeDtypeStruct((bsz, seq, PZ_WIDTH), F32)] + out_shape
    outs = pl.pallas_call(
        functools.partial(_inproj_kernel, rope=rope_tabs is not None, kv_only=kv_only,
                          sub=INPROJ_SUB_ROWS),
        grid=(bsz, seq // tm),
        in_specs=in_specs,
        out_specs=out_specs,
        out_shape=out_shape,
        compiler_params=_cparams(2),
        name="mixer_in_proj",
    )(*args)
    return [o.reshape(lead + o.shape[2:]) for o in outs]


def _scan_kernel(kx_ref, vx_ref, ky_ref, vy_ref, wk_ref, cd_ref, sx_ref, sy_ref, kvx_ref, kvy_ref):
    n_x = kx_ref.shape[1] // CHUNK
    n_y = ky_ref.shape[1] // CHUNK

    def increments(k_ref, v_ref, kv_ref, n_chunks):
        unroll = min(SCAN_UNROLL, n_chunks)
        assert n_chunks % unroll == 0

        def body(i, carry):
            for u in range(unroll):
                n = i * unroll + u
                rows = pl.ds(pl.multiple_of(n * CHUNK, CHUNK), CHUNK)
                for hd in range(RET_HEADS):
                    lanes = slice(hd * HEAD_DIM, (hd + 1) * HEAD_DIM)
                    v = v_ref[0, rows, lanes].astype(F32)
                    vw = (jnp.concatenate([v, v], axis=-1) * wk_ref[hd]).astype(BF16)
                    kv_ref[n, hd] = lax.dot_general(k_ref[0, rows, lanes], vw, (((0,), (0,)), ((), ())),
                                                    preferred_element_type=F32)
            return carry
        lax.fori_loop(0, n_chunks // unroll, body, 0)

    increments(ky_ref, vy_ref, kvy_ref, n_y)
    increments(kx_ref, vx_ref, kvx_ref, n_x)

    for hd in range(RET_HEADS):
        for direction in (0, 1):
            lanes = slice(direction * HEAD_DIM, (direction + 1) * HEAD_DIM)
            decay = cd_ref[direction:direction + 1, hd * HEAD_DIM:(hd + 1) * HEAD_DIM]

            def scan(state, kv_ref, s_ref, n_chunks):
                def body(i, st):
                    n = i if direction == 0 else n_chunks - 1 - i
                    s_ref[0, n, hd, :, lanes] = st.astype(BF16)
                    return decay * st + kv_ref[n, hd, :, lanes]
                return lax.fori_loop(0, n_chunks, body, state)

            state = scan(jnp.zeros((HEAD_DIM, HEAD_DIM), F32), kvy_ref, sy_ref, n_y)
            scan(state, kvx_ref, sx_ref, n_x)


def _state_scan(kx, vx, ky, vy, wk, cd):
    bsz, seq, _ = kx.shape
    ctx = ky.shape[1]
    tok = lambda n: pl.BlockSpec((1, n, RET_WIDTH), lambda b: (b, 0, 0))
    st = lambda n: pl.BlockSpec((1, n, RET_HEADS, HEAD_DIM, 2 * HEAD_DIM), lambda b: (b, 0, 0, 0, 0))
    return pl.pallas_call(
        _scan_kernel,
        grid=(bsz,),
        in_specs=[tok(seq), tok(seq), tok(ctx), tok(ctx),
                  _const_spec((RET_HEADS, CHUNK, 2 * HEAD_DIM)), _const_spec((2, RET_WIDTH))],
        out_specs=[st(seq // CHUNK), st(ctx // CHUNK)],
        out_shape=[jax.ShapeDtypeStruct((bsz, seq // CHUNK, RET_HEADS, HEAD_DIM, 2 * HEAD_DIM), BF16),
                   jax.ShapeDtypeStruct((bsz, ctx // CHUNK, RET_HEADS, HEAD_DIM, 2 * HEAD_DIM), BF16)],
        scratch_shapes=[pltpu.VMEM((seq // CHUNK, RET_HEADS, HEAD_DIM, 2 * HEAD_DIM), F32),
                        pltpu.VMEM((ctx // CHUNK, RET_HEADS, HEAD_DIM, 2 * HEAD_DIM), F32)],
        compiler_params=_cparams(1),
        name="retention_state_scan",
    )(kx, vx, ky, vy, wk, cd)


def _mixer_rows(c, t, tb, seq_len, q_ref, k_ref, v_ref, g_ref, s_ref, dmask_ref, wq_ref,
                poolw_ref, pscale_ref, gng_ref, dw_ref, db_ref, lng_ref, lnb_ref, ext_ref,
                sh_ref, pool_ref):
    r0 = c * CHUNK
    rows = slice(r0, r0 + CHUNK)

    lane = lax.broadcasted_iota(jnp.int32, (CHUNK, LANES), 1)
    pos = lax.broadcasted_iota(jnp.int32, (CHUNK, LANES), 0) + (t * tb + r0)
    low_half = lane < POOL_GROUP
    pooled = []
    for col, (w_lo, w_hi) in enumerate(((2, 4), (8, 16))):
        lanes = slice(col * LANES, (col + 1) * LANES)
        half = jnp.where(low_half, w_lo // 2, w_hi // 2)
        cnt = jnp.minimum(pos + half, seq_len) - jnp.maximum(pos - half, 0)
        pooled.append(pool_ref[rows, lanes] / cnt.astype(F32)
                      - ext_ref[HALO + r0:HALO + r0 + CHUNK, lanes])
    pooled = jnp.concatenate(pooled, axis=-1).astype(BF16)
    mixed = jnp.dot(pooled, poolw_ref[...], preferred_element_type=F32) * pscale_ref[...]
    pieces = [mixed.astype(BF16)]

    for hd in range(RET_HEADS):
        lanes = slice(hd * HEAD_DIM, (hd + 1) * HEAD_DIM)
        qh = q_ref[0, rows, lanes]
        kh = k_ref[0, rows, lanes]
        vh = v_ref[0, rows, lanes]
        sc = lax.dot_general(qh, kh, (((1,), (1,)), ((), ())), preferred_element_type=F32)
        prob = (sc * dmask_ref[hd]).astype(BF16)
        o = jnp.dot(prob, vh, preferred_element_type=F32)
        cross = jnp.dot(qh, s_ref[0, c, hd], preferred_element_type=F32) * wq_ref[hd]
        o = o + cross[:, :HEAD_DIM] + cross[:, HEAD_DIM:]
        mu = jnp.mean(o, axis=-1, keepdims=True)
        cen = o - mu
        var = jnp.mean(cen * cen, axis=-1, keepdims=True)
        y = cen * lax.rsqrt(var + EPS) * gng_ref[:, lanes]
        y = y * _silu(g_ref[0, rows, lanes].astype(F32))
        pieces.append(y.astype(BF16))

    rb = 64
    for b0 in range(r0, r0 + CHUNK, rb):
        acc = jnp.zeros((rb, CONV_WIDTH), F32) + db_ref[...]
        for kk in range(CONV_K):
            lo = b0 + kk + HALO - CONV_K // 2
            r = lo % SUBLANES
            if r == 0:
                src = ext_ref[lo:lo + rb, POOL_WIDTH:]
            else:
                src = sh_ref[r - 1, lo - r:lo - r + rb, :]
            acc = acc + dw_ref[kk:kk + 1, :] * src
        mu = jnp.mean(acc, axis=-1, keepdims=True)
        cen = acc - mu
        var = jnp.mean(cen * cen, axis=-1, keepdims=True)
        zn = cen * lax.rsqrt(var + EPS) * lng_ref[...] + lnb_ref[...]
        pieces.append(_silu(zn).astype(BF16))
    conv = jnp.concatenate(pieces[1 + RET_HEADS:], axis=0)
    return jnp.concatenate(pieces[:1 + RET_HEADS] + [conv], axis=-1)


def _mix_kernel(x_ref, mod_ref, pz_ref, pzp_ref, pzn_ref, q_ref, k_ref, v_ref, g_ref, s_ref,
                dmask_ref, wq_ref, poolw_ref, pscale_ref, gng_ref, dw_ref, db_ref, lng_ref,
                lnb_ref, wout_ref, o_ref, ext_ref, sh_ref, pool_ref, *, seq_len, tb):
    t = pl.program_id(1)
    n_t = pl.num_programs(1)

    ext_ref[0:HALO, :] = jnp.where(t > 0, pzp_ref[0], 0.0)
    ext_ref[HALO:HALO + tb, :] = pz_ref[0]
    ext_ref[HALO + tb:, :] = jnp.where(t < n_t - 1, pzn_ref[0], 0.0)
    n_sh = sh_ref.shape[1]
    for r in range(1, SUBLANES):
        sh_ref[r - 1] = ext_ref[r:r + n_sh, POOL_WIDTH:]

    n_ext = tb + 2 * HALO

    def pair(w, s):
        return pltpu.roll(w, s, 0) + pltpu.roll(w, n_ext - s, 0)

    low_half = lax.broadcasted_iota(jnp.int32, (n_ext, LANES), 1) < POOL_GROUP
    for col in range(POOL_WIDTH // LANES):
        lanes = slice(col * LANES, (col + 1) * LANES)
        e = ext_ref[:, lanes]
        w_lo = e + pltpu.roll(e, 1, 0)
        w_hi = pair(w_lo, 1)
        for _ in range(col):
            w_lo = pair(w_hi, 2)
            w_hi = pair(w_lo, 4)
        pool_ref[:, lanes] = jnp.where(low_half, w_lo, w_hi)[HALO:HALO + tb]

    def heads(c):
        return _mixer_rows(c, t, tb, seq_len, q_ref, k_ref, v_ref, g_ref, s_ref, dmask_ref, wq_ref,
                           poolw_ref, pscale_ref, gng_ref, dw_ref, db_ref, lng_ref, lnb_ref, ext_ref,
                           sh_ref, pool_ref)

    n_chunks = tb // CHUNK
    cat = heads(0)
    for c in range(n_chunks):
        cat_next = heads(c + 1) if c + 1 < n_chunks else None
        rows = slice(c * CHUNK, (c + 1) * CHUNK)
        out = jnp.dot(cat, wout_ref[...], preferred_element_type=F32)
        o_ref[0, rows, :] = x_ref[0, rows, :] + mod_ref[5:6, :] * out
        cat = cat_next


def _mixer_body(x, mod_all, pz, q, k, v, g, states, dmask, wq, poolw, pscale, gng, dw, db, lng, lnb,
                w_out, *, layer, mod_row, tb=MIX_ROWS):
    bsz, seq, _ = x.shape
    tb = min(tb, seq)
    n_halo = seq // HALO
    hb = tb // HALO
    tok = lambda w: pl.BlockSpec((1, tb, w), lambda b, t: (b, t, 0))
    in_specs = [
        tok(D_MODEL),
        _mod_spec(layer, mod_row),
        tok(PZ_WIDTH),
        pl.BlockSpec((1, HALO, PZ_WIDTH), lambda b, t: (b, jnp.maximum(t * hb - 1, 0), 0)),
        pl.BlockSpec((1, HALO, PZ_WIDTH), lambda b, t: (b, jnp.minimum((t + 1) * hb, n_halo - 1), 0)),
        tok(RET_WIDTH), tok(RET_WIDTH), tok(RET_WIDTH), tok(RET_WIDTH),
        pl.BlockSpec((1, tb // CHUNK, RET_HEADS, HEAD_DIM, 2 * HEAD_DIM), lambda b, t: (b, t, 0, 0, 0)),
        _const_spec((RET_HEADS, CHUNK, CHUNK)),
        _const_spec((RET_HEADS, CHUNK, 2 * HEAD_DIM)),
        _const_spec((POOL_WIDTH, POOL_WIDTH)),
        _const_spec((1, POOL_WIDTH)),
        _const_spec((1, RET_WIDTH)),
        _const_spec((CONV_K + 1, CONV_WIDTH)),
        _const_spec((1, CONV_WIDTH)),
        _const_spec((1, CONV_WIDTH)),
        _const_spec((1, CONV_WIDTH)),
        _const_spec((D_MODEL, D_MODEL), (layer,)),
    ]
    return pl.pallas_call(
        functools.partial(_mix_kernel, seq_len=seq, tb=tb),
        grid=(bsz, seq // tb),
        in_specs=in_specs,
        out_specs=tok(D_MODEL),
        out_shape=jax.ShapeDtypeStruct(x.shape, F32),
        scratch_shapes=[pltpu.VMEM((tb + 2 * HALO, PZ_WIDTH), F32),
                        pltpu.VMEM((SUBLANES - 1, tb + 2 * HALO - SUBLANES, CONV_WIDTH), F32),
                        pltpu.VMEM((tb, POOL_WIDTH), F32)],
        compiler_params=_cparams(2),
        name="mixer_body",
    )(x, mod_all, pz, pz, pz, q, k, v, g, states, dmask, wq, poolw, pscale, gng, dw, db, lng, lnb,
      w_out)


def _rope_tables(seq):
    n_freq = HEAD_DIM // 4
    inv = ROPE_BASE ** (-jnp.arange(n_freq, dtype=F32) / n_freq)
    rows = seq // GRID_W
    row = jnp.repeat(jnp.arange(rows, dtype=F32), GRID_W)
    col = jnp.tile(jnp.arange(GRID_W, dtype=F32), rows)
    ang_r = row[:, None] * inv[None]
    ang_c = col[:, None] * inv[None]
    cos = jnp.concatenate([jnp.cos(ang_r), jnp.cos(ang_c)] * 2, axis=-1)
    sin = jnp.concatenate([-jnp.sin(ang_r), -jnp.sin(ang_c), jnp.sin(ang_r), jnp.sin(ang_c)], axis=-1)
    return cos, sin


def _permute_heads(w_in):
    def perm(w):
        w = w.reshape(w.shape[:-1] + (RET_HEADS, 2, 2, HEAD_DIM // 4))
        return jnp.swapaxes(w, -2, -3).reshape(w.shape[:-4] + (RET_WIDTH,))
    return jnp.concatenate([w_in[..., :Q_OFF], perm(w_in[..., Q_OFF:K_OFF]),
                            perm(w_in[..., K_OFF:V_OFF]), w_in[..., V_OFF:]], axis=-1)


def _retention_tables(dec_f, dec_b):
    lg_f = jax.nn.log_sigmoid(dec_f.astype(F32))
    lg_b = jax.nn.log_sigmoid(dec_b.astype(F32))
    pos = jnp.arange(CHUNK, dtype=F32)
    diff = pos[:, None] - pos[None, :]
    dmask = jnp.where(diff[None] >= 0,
                      jnp.exp(jnp.maximum(diff, 0.0)[None] * lg_f[:, None, None]),
                      jnp.exp(jnp.maximum(-diff, 0.0)[None] * lg_b[:, None, None]))
    ones = jnp.ones((1, 1, HEAD_DIM), F32)
    wq_f = jnp.exp((pos + 1.0)[None, :] * lg_f[:, None])[:, :, None] * ones
    wq_b = jnp.exp((CHUNK - pos)[None, :] * lg_b[:, None])[:, :, None] * ones
    wq = jnp.concatenate([wq_f, wq_b], axis=-1)
    wk_f = jnp.exp((CHUNK - 1 - pos)[None, :] * lg_f[:, None])[:, :, None] * ones
    wk_b = jnp.exp(pos[None, :] * lg_b[:, None])[:, :, None] * ones
    wk = jnp.concatenate([wk_f, wk_b], axis=-1)
    cd = jnp.stack([jnp.repeat(jnp.exp(CHUNK * lg_f), HEAD_DIM),
                    jnp.repeat(jnp.exp(CHUNK * lg_b), HEAD_DIM)])
    return dmask, wq, wk, cd


def _block_diag(pool_w):
    out = jnp.zeros((POOL_WIDTH, POOL_WIDTH), pool_w.dtype)
    for gi in range(len(POOL_WINDOWS)):
        out = out.at[gi * POOL_GROUP:(gi + 1) * POOL_GROUP, gi * POOL_GROUP:(gi + 1) * POOL_GROUP].set(pool_w[gi])
    return out


def kernel(x, c, ctx, c_ctx, w_mod, b_mod, norm_g, ffn_w1, ffn_w3, ffn_w2, w_in, w_out, pool_w,
           pool_scale, ret_decay_fwd, ret_decay_bwd, ret_gn_g, conv_dw, conv_b, conv_ln_g,
           conv_ln_b, final_g):
    depth = w_mod.shape[0]
    bsz, seq, _ = x.shape
    cvec = jnp.zeros((MOD_ROWS, D_MODEL), F32).at[:bsz].set(c).at[bsz].set(c_ctx)
    mod_all = _modulation(cvec, w_mod, b_mod).reshape(depth, MOD_ROWS, N_MOD, D_MODEL)
    latent_row = lambda b: b
    context_row = lambda b: bsz
    rope_tabs = _rope_tables(seq)
    w1 = ffn_w1.astype(BF16)
    w3 = ffn_w3.astype(BF16)
    w2 = ffn_w2.astype(BF16)
    win = _permute_heads(w_in).astype(BF16)
    wout = w_out.astype(BF16)
    y = ctx
    for l in range(depth):
        last = l == depth - 1
        x, y = _ffn(x, y, mod_all, norm_g, w1, w3, w2, layer=l, half=0)

        dmask, wq, wk, cd = _retention_tables(ret_decay_fwd[l], ret_decay_bwd[l])
        pzx, qx, kx, vx, gx = _inproj(x, mod_all, norm_g, win, rope_tabs, layer=l, mod_row=latent_row,
                                      flatten=False)
        if last:
            ky, vy = _inproj(y, mod_all, norm_g, win, None, layer=l, mod_row=context_row,
                             flatten=True, kv_only=True)
        else:
            pzy, qy, ky, vy, gy = _inproj(y, mod_all, norm_g, win, None, layer=l, mod_row=context_row,
                                          flatten=True)
        sx, sy = _state_scan(kx, vx, ky, vy, wk, cd)
        params = (dmask, wq, _block_diag(pool_w[l]).astype(BF16), pool_scale[l].reshape(1, -1),
                  ret_gn_g[l].reshape(1, -1),
                  jnp.concatenate([conv_dw[l], jnp.zeros((1, CONV_WIDTH), F32)], axis=0),
                  conv_b[l].reshape(1, -1), conv_ln_g[l].reshape(1, -1), conv_ln_b[l].reshape(1, -1),
                  wout)
        x = _mixer_body(x, mod_all, pzx, qx, kx, vx, gx, sx, *params, layer=l, mod_row=latent_row)
        if last:
            x, _ = _ffn(x, None, mod_all, norm_g, w1, w3, w2, layer=l, half=1, final_g=final_g)
        else:
            y = _mixer_body(y, mod_all, pzy, qy, ky, vy, gy, sy, *params, layer=l, mod_row=context_row)
            x, y = _ffn(x, y, mod_all, norm_g, w1, w3, w2, layer=l, half=1)
    return x
```

```python
import functools

import jax
import jax.numpy as jnp
from jax import lax
from jax.experimental import pallas as pl
from jax.experimental.pallas import tpu as pltpu

F32 = jnp.float32
BF16 = jnp.bfloat16

D_MODEL = 1024
N_MOD = 9
D_FF = 2816
POOL_WIDTH = 256
POOL_WINDOWS = (2, 4, 8, 16)
POOL_GROUP = 64
RET_WIDTH = 512
RET_HEADS = 4
HEAD_DIM = 128
CHUNK = 128
CONV_WIDTH = 256
CONV_K = 31
IN_WIDTH = 2816
Q_OFF = POOL_WIDTH
K_OFF = Q_OFF + RET_WIDTH
V_OFF = K_OFF + RET_WIDTH
G_OFF = V_OFF + RET_WIDTH
C_OFF = G_OFF + RET_WIDTH
GRID_W = 64
ROPE_BASE = 10000.0
EPS = 1e-6

SUBLANES = 8
LANES = 128
FFN_ROWS = 512
FFN_SUB_ROWS = 128
INPROJ_ROWS = 1024
INPROJ_SUB_ROWS = 128
MIX_ROWS = 512
SCAN_UNROLL = 4
HALO = 16
PZ_WIDTH = POOL_WIDTH + CONV_WIDTH
MOD_ROWS = 24
VMEM_LIMIT = 56 * 1024 * 1024


def _cparams(n_axes):
    return pltpu.CompilerParams(dimension_semantics=("arbitrary",) * n_axes,
                                vmem_limit_bytes=VMEM_LIMIT)


def _const_spec(shape, lead=()):
    idx = tuple(lead) + (0,) * len(shape)
    return pl.BlockSpec((None,) * len(lead) + tuple(shape), lambda *_: idx,
                        pipeline_mode=pl.Buffered(1))


def _sigmoid(x):
    return 1.0 / (1.0 + jnp.exp(-x))


def _silu(x):
    return x * _sigmoid(x)


def _rms_mod(x, g, shift, scale):
    y = x * lax.rsqrt(jnp.mean(x * x, axis=-1, keepdims=True) + EPS) * g
    return y * (1.0 + scale) + shift


def _mod_kernel(c_ref, w_ref, b_ref, o_ref):
    s = _silu(c_ref[...]).astype(BF16)
    o_ref[0] = jnp.dot(s, w_ref[0].astype(BF16), preferred_element_type=F32) + b_ref[0]


def _modulation(cvec, w_mod, b_mod):
    depth = w_mod.shape[0]
    tn = 1152
    return pl.pallas_call(
        _mod_kernel,
        grid=(depth, N_MOD * D_MODEL // tn),
        in_specs=[pl.BlockSpec((MOD_ROWS, D_MODEL), lambda l, j: (0, 0)),
                  pl.BlockSpec((1, D_MODEL, tn), lambda l, j: (l, 0, j)),
                  pl.BlockSpec((1, 1, tn), lambda l, j: (l, 0, j))],
        out_specs=pl.BlockSpec((1, MOD_ROWS, tn), lambda l, j: (l, 0, j)),
        out_shape=jax.ShapeDtypeStruct((depth, MOD_ROWS, N_MOD * D_MODEL), F32),
        compiler_params=_cparams(2),
        name="modulation",
    )(cvec, w_mod, b_mod.reshape(depth, 1, N_MOD * D_MODEL))


def _mod_spec(layer, mod_row):
    return pl.BlockSpec((None, None, N_MOD, D_MODEL), lambda b, *_: (layer, mod_row(b), 0, 0))


def _ffn_kernel(*refs, half, sub, n_x, has_ctx, final):
    refs = list(refs)
    x_ref = refs.pop(0)
    y_ref = refs.pop(0) if has_ctx else None
    mod_ref, g_ref, w1_ref, w3_ref, w2_ref = refs[:5]
    fg_ref = refs[5] if final else None
    ox_ref = refs[-2] if has_ctx else refs[-1]
    oy_ref = refs[-1] if has_ctx else None

    shift = mod_ref[6 * half:6 * half + 1, :]
    scale = mod_ref[6 * half + 1:6 * half + 2, :]
    gate = mod_ref[6 * half + 2:6 * half + 3, :]

    def run(src_ref, dst_ref, fg_ref):
        for r0 in range(0, src_ref.shape[1], sub):
            rows = slice(r0, r0 + sub)
            x = src_ref[0, rows, :]
            h = _rms_mod(x, g_ref[2 * half:2 * half + 1, :], shift, scale).astype(BF16)
            a = jnp.dot(h, w1_ref[...], preferred_element_type=F32)
            b = jnp.dot(h, w3_ref[...], preferred_element_type=F32)
            u = (_silu(a) * b).astype(BF16)
            y = x + 0.5 * gate * jnp.dot(u, w2_ref[...], preferred_element_type=F32)
            if fg_ref is not None:
                y = y * lax.rsqrt(jnp.mean(y * y, axis=-1, keepdims=True) + EPS) * fg_ref[...]
            dst_ref[0, rows, :] = y

    if has_ctx:
        i = pl.program_id(0)
        pl.when(i < n_x)(lambda: run(x_ref, ox_ref, fg_ref))
        pl.when(i >= n_x)(lambda: run(y_ref, oy_ref, None))
    else:
        run(x_ref, ox_ref, fg_ref)


def _ffn(x, y, mod_all, norm_g, w1, w3, w2, *, layer, half, final_g=None, tm=FFN_ROWS):
    bsz, seq, _ = x.shape
    n_t = seq // tm
    n_x = bsz * n_t
    has_ctx = y is not None
    x_idx = lambda i: jnp.minimum(i, n_x - 1)
    x_spec = pl.BlockSpec((1, tm, D_MODEL), lambda i: (x_idx(i) // n_t, x_idx(i) % n_t, 0))
    y_spec = pl.BlockSpec((1, tm, D_MODEL), lambda i: (0, jnp.maximum(i - n_x, 0), 0))
    mod_spec = pl.BlockSpec((None, None, N_MOD, D_MODEL),
                            lambda i: (layer, jnp.where(i < n_x, x_idx(i) // n_t, bsz), 0, 0))
    in_specs, args = [x_spec], [x]
    out_specs, out_shape = [x_spec], [jax.ShapeDtypeStruct(x.shape, F32)]
    n_y = 0
    if has_ctx:
        y_flat = y.reshape(1, -1, D_MODEL)
        n_y = y_flat.shape[1] // tm
        in_specs.append(y_spec)
        args.append(y_flat)
        out_specs.append(y_spec)
        out_shape.append(jax.ShapeDtypeStruct(y_flat.shape, F32))
    in_specs += [mod_spec,
                 _const_spec((3, D_MODEL), (layer,)),
                 _const_spec((D_MODEL, D_FF), (layer, half)),
                 _const_spec((D_MODEL, D_FF), (layer, half)),
                 _const_spec((D_FF, D_MODEL), (layer, half))]
    args += [mod_all, norm_g, w1, w3, w2]
    if final_g is not None:
        in_specs.append(_const_spec((1, D_MODEL)))
        args.append(final_g.reshape(1, D_MODEL))
    outs = pl.pallas_call(
        functools.partial(_ffn_kernel, half=half, sub=FFN_SUB_ROWS, n_x=n_x, has_ctx=has_ctx,
                          final=final_g is not None),
        grid=(n_x + n_y,),
        in_specs=in_specs,
        out_specs=out_specs,
        out_shape=out_shape,
        compiler_params=_cparams(1),
        name="swiglu_half_step",
    )(*args)
    return (outs[0], outs[1].reshape(y.shape)) if has_ctx else (outs[0], None)


def _rope(t, cos, sin):
    return t * cos + pltpu.roll(t, HEAD_DIM // 2, 1) * sin


def _inproj_kernel(x_ref, mod_ref, g_ref, w_ref, *rest, rope, kv_only, sub):
    if kv_only:
        k_ref, v_ref = rest
    elif rope:
        cos_ref, sin_ref, pz_ref, q_ref, k_ref, v_ref, go_ref = rest
    else:
        pz_ref, q_ref, k_ref, v_ref, go_ref = rest
    k_scale = HEAD_DIM ** -0.5
    for r0 in range(0, x_ref.shape[1], sub):
        rows = slice(r0, r0 + sub)
        h = _rms_mod(x_ref[0, rows, :], g_ref[1:2, :], mod_ref[3:4, :], mod_ref[4:5, :]).astype(BF16)

        lo_all, hi_all = (K_OFF, G_OFF) if kv_only else (0, IN_WIDTH)
        p = jnp.dot(h, w_ref[:, lo_all:hi_all], preferred_element_type=F32)

        def proj(lo, hi):
            return p[:, lo - lo_all:hi - lo_all]

        heads = ((K_OFF, k_ref, k_scale),) if kv_only else ((Q_OFF, q_ref, None), (K_OFF, k_ref, k_scale))
        for off, dst, scale in heads:
            t = proj(off, off + RET_WIDTH)
            for hd in range(RET_HEADS):
                lanes = slice(hd * HEAD_DIM, (hd + 1) * HEAD_DIM)
                th = t[:, lanes]
                if rope:
                    th = _rope(th, cos_ref[rows, :], sin_ref[rows, :])
                if scale is not None:
                    th = th * scale
                dst[0, rows, lanes] = th.astype(BF16)
        v_ref[0, rows, :] = proj(V_OFF, G_OFF).astype(BF16)
        if kv_only:
            continue
        go_ref[0, rows, :] = proj(G_OFF, C_OFF).astype(BF16)
        pz_ref[0, rows, :POOL_WIDTH] = proj(0, POOL_WIDTH)
        glu = proj(C_OFF, IN_WIDTH)
        pz_ref[0, rows, POOL_WIDTH:] = glu[:, :CONV_WIDTH] * _sigmoid(glu[:, CONV_WIDTH:])


def _inproj(x, mod_all, norm_g, w_in, rope_tabs, *, layer, mod_row, flatten, kv_only=False,
            tm=INPROJ_ROWS):
    lead = x.shape[:2]
    if flatten:
        assert rope_tabs is None
        x = x.reshape(1, -1, D_MODEL)
    assert flatten or not kv_only
    bsz, seq, _ = x.shape
    in_specs = [pl.BlockSpec((1, tm, D_MODEL), lambda b, t: (b, t, 0)),
                _mod_spec(layer, mod_row),
                _const_spec((3, D_MODEL), (layer,)),
                _const_spec((D_MODEL, IN_WIDTH), (layer,))]
    args = [x, mod_all, norm_g, w_in]
    if rope_tabs is not None:
        in_specs += [pl.BlockSpec((tm, HEAD_DIM), lambda b, t: (t, 0))] * 2
        args += list(rope_tabs)
    tok = lambda w: pl.BlockSpec((1, tm, w), lambda b, t: (b, t, 0))
    n_ret = 2 if kv_only else 4
    out_specs = [tok(RET_WIDTH)] * n_ret
    out_shape = [jax.ShapeDtypeStruct((bsz, seq, RET_WIDTH), BF16)] * n_ret
    if not kv_only:
        out_specs = [tok(PZ_WIDTH)] + out_specs
        out_shape = [jax.ShapeDtypeStruct((bsz, seq, PZ_WIDTH), F32)] + out_shape
    outs = pl.pallas_call(
        functools.partial(_inproj_kernel, rope=rope_tabs is not None, kv_only=kv_only,
                          sub=INPROJ_SUB_ROWS),
        grid=(bsz, seq // tm),
        in_specs=in_specs,
        out_specs=out_specs,
        out_shape=out_shape,
        compiler_params=_cparams(2),
        name="mixer_in_proj",
    )(*args)
    return [o.reshape(lead + o.shape[2:]) for o in outs]


def _scan_kernel(kx_ref, vx_ref, ky_ref, vy_ref, wk_ref, cd_ref, sx_ref, sy_ref, kvx_ref, kvy_ref):
    n_x = kx_ref.shape[1] // CHUNK
    n_y = ky_ref.shape[1] // CHUNK

    def increments(k_ref, v_ref, kv_ref, n_chunks):
        unroll = min(SCAN_UNROLL, n_chunks)
        assert n_chunks % unroll == 0

        def body(i, carry):
            for u in range(unroll):
                n = i * unroll + u
                rows = pl.ds(pl.multiple_of(n * CHUNK, CHUNK), CHUNK)
                for hd in range(RET_HEADS):
                    lanes = slice(hd * HEAD_DIM, (hd + 1) * HEAD_DIM)
                    v = v_ref[0, rows, lanes].astype(F32)
                    vw = (jnp.concatenate([v, v], axis=-1) * wk_ref[hd]).astype(BF16)
                    kv_ref[n, hd] = lax.dot_general(k_ref[0, rows, lanes], vw, (((0,), (0,)), ((), ())),
                                                    preferred_element_type=F32)
            return carry
        lax.fori_loop(0, n_chunks // unroll, body, 0)

    increments(ky_ref, vy_ref, kvy_ref, n_y)
    increments(kx_ref, vx_ref, kvx_ref, n_x)

    for hd in range(RET_HEADS):
        for direction in (0, 1):
            lanes = slice(direction * HEAD_DIM, (direction + 1) * HEAD_DIM)
            decay = cd_ref[direction:direction + 1, hd * HEAD_DIM:(hd + 1) * HEAD_DIM]

            def scan(state, kv_ref, s_ref, n_chunks):
                def body(i, st):
                    n = i if direction == 0 else n_chunks - 1 - i
                    s_ref[0, n, hd, :, lanes] = st.astype(BF16)
                    return decay * st + kv_ref[n, hd, :, lanes]
                return lax.fori_loop(0, n_chunks, body, state)

            state = scan(jnp.zeros((HEAD_DIM, HEAD_DIM), F32), kvy_ref, sy_ref, n_y)
            scan(state, kvx_ref, sx_ref, n_x)


def _state_scan(kx, vx, ky, vy, wk, cd):
    bsz, seq, _ = kx.shape
    ctx = ky.shape[1]
    tok = lambda n: pl.BlockSpec((1, n, RET_WIDTH), lambda b: (b, 0, 0))
    st = lambda n: pl.BlockSpec((1, n, RET_HEADS, HEAD_DIM, 2 * HEAD_DIM), lambda b: (b, 0, 0, 0, 0))
    return pl.pallas_call(
        _scan_kernel,
        grid=(bsz,),
        in_specs=[tok(seq), tok(seq), tok(ctx), tok(ctx),
                  _const_spec((RET_HEADS, CHUNK, 2 * HEAD_DIM)), _const_spec((2, RET_WIDTH))],
        out_specs=[st(seq // CHUNK), st(ctx // CHUNK)],
        out_shape=[jax.ShapeDtypeStruct((bsz, seq // CHUNK, RET_HEADS, HEAD_DIM, 2 * HEAD_DIM), BF16),
                   jax.ShapeDtypeStruct((bsz, ctx // CHUNK, RET_HEADS, HEAD_DIM, 2 * HEAD_DIM), BF16)],
        scratch_shapes=[pltpu.VMEM((seq // CHUNK, RET_HEADS, HEAD_DIM, 2 * HEAD_DIM), F32),
                        pltpu.VMEM((ctx // CHUNK, RET_HEADS, HEAD_DIM, 2 * HEAD_DIM), F32)],
        compiler_params=_cparams(1),
        name="retention_state_scan",
    )(kx, vx, ky, vy, wk, cd)


def _mixer_rows(c, t, tb, seq_len, q_ref, k_ref, v_ref, g_ref, s_ref, dmask_ref, wq_ref,
                poolw_ref, pscale_ref, gng_ref, dw_ref, db_ref, lng_ref, lnb_ref, ext_ref,
                sh_ref, pool_ref):
    r0 = c * CHUNK
    rows = slice(r0, r0 + CHUNK)

    lane = lax.broadcasted_iota(jnp.int32, (CHUNK, LANES), 1)
    pos = lax.broadcasted_iota(jnp.int32, (CHUNK, LANES), 0) + (t * tb + r0)
    low_half = lane < POOL_GROUP
    pooled = []
    for col, (w_lo, w_hi) in enumerate(((2, 4), (8, 16))):
        lanes = slice(col * LANES, (col + 1) * LANES)
        half = jnp.where(low_half, w_lo // 2, w_hi // 2)
        cnt = jnp.minimum(pos + half, seq_len) - jnp.maximum(pos - half, 0)
        pooled.append(pool_ref[rows, lanes] / cnt.astype(F32)
                      - ext_ref[HALO + r0:HALO + r0 + CHUNK, lanes])
    pooled = jnp.concatenate(pooled, axis=-1).astype(BF16)
    mixed = jnp.dot(pooled, poolw_ref[...], preferred_element_type=F32) * pscale_ref[...]
    pieces = [mixed.astype(BF16)]

    for hd in range(RET_HEADS):
        lanes = slice(hd * HEAD_DIM, (hd + 1) * HEAD_DIM)
        qh = q_ref[0, rows, lanes]
        kh = k_ref[0, rows, lanes]
        vh = v_ref[0, rows, lanes]
        sc = lax.dot_general(qh, kh, (((1,), (1,)), ((), ())), preferred_element_type=F32)
        prob = (sc * dmask_ref[hd]).astype(BF16)
        o = jnp.dot(prob, vh, preferred_element_type=F32)
        cross = jnp.dot(qh, s_ref[0, c, hd], preferred_element_type=F32) * wq_ref[hd]
        o = o + cross[:, :HEAD_DIM] + cross[:, HEAD_DIM:]
        mu = jnp.mean(o, axis=-1, keepdims=True)
        cen = o - mu
        var = jnp.mean(cen * cen, axis=-1, keepdims=True)
        y = cen * lax.rsqrt(var + EPS) * gng_ref[:, lanes]
        y = y * _silu(g_ref[0, rows, lanes].astype(F32))
        pieces.append(y.astype(BF16))

    rb = 64
    for b0 in range(r0, r0 + CHUNK, rb):
        acc = jnp.zeros((rb, CONV_WIDTH), F32) + db_ref[...]
        for kk in range(CONV_K):
            lo = b0 + kk + HALO - CONV_K // 2
            r = lo % SUBLANES
            if r == 0:
                src = ext_ref[lo:lo + rb, POOL_WIDTH:]
            else:
                src = sh_ref[r - 1, lo - r:lo - r + rb, :]
            acc = acc + dw_ref[kk:kk + 1, :] * src
        mu = jnp.mean(acc, axis=-1, keepdims=True)
        cen = acc - mu
        var = jnp.mean(cen * cen, axis=-1, keepdims=True)
        zn = cen * lax.rsqrt(var + EPS) * lng_ref[...] + lnb_ref[...]
        pieces.append(_silu(zn).astype(BF16))
    conv = jnp.concatenate(pieces[1 + RET_HEADS:], axis=0)
    return jnp.concatenate(pieces[:1 + RET_HEADS] + [conv], axis=-1)


def _mix_kernel(x_ref, mod_ref, pz_ref, pzp_ref, pzn_ref, q_ref, k_ref, v_ref, g_ref, s_ref,
                dmask_ref, wq_ref, poolw_ref, pscale_ref, gng_ref, dw_ref, db_ref, lng_ref,
                lnb_ref, wout_ref, o_ref, ext_ref, sh_ref, pool_ref, *, seq_len, tb):
    t = pl.program_id(1)
    n_t = pl.num_programs(1)

    ext_ref[0:HALO, :] = jnp.where(t > 0, pzp_ref[0], 0.0)
    ext_ref[HALO:HALO + tb, :] = pz_ref[0]
    ext_ref[HALO + tb:, :] = jnp.where(t < n_t - 1, pzn_ref[0], 0.0)
    n_sh = tb + 2 * HALO - SUBLANES
    for r in range(1, SUBLANES):
        sh_ref[r - 1, 0:n_sh, :] = ext_ref[r:r + n_sh, POOL_WIDTH:]

    n_ext = tb + 2 * HALO

    def pair(w, s):
        return pltpu.roll(w, s, 0) + pltpu.roll(w, n_ext - s, 0)

    low_half = lax.broadcasted_iota(jnp.int32, (n_ext, LANES), 1) < POOL_GROUP
    for col in range(POOL_WIDTH // LANES):
        lanes = slice(col * LANES, (col + 1) * LANES)
        e = ext_ref[:, lanes]
        w_lo = e + pltpu.roll(e, 1, 0)
        w_hi = pair(w_lo, 1)
        for _ in range(col):
            w_lo = pair(w_hi, 2)
            w_hi = pair(w_lo, 4)
        pool_ref[:, lanes] = jnp.where(low_half, w_lo, w_hi)[HALO:HALO + tb]

    def heads(c):
        return _mixer_rows(c, t, tb, seq_len, q_ref, k_ref, v_ref, g_ref, s_ref, dmask_ref, wq_ref,
                           poolw_ref, pscale_ref, gng_ref, dw_ref, db_ref, lng_ref, lnb_ref, ext_ref,
                           sh_ref, pool_ref)

    n_chunks = tb // CHUNK
    cat = heads(0)
    for c in range(n_chunks):
        cat_next = heads(c + 1) if c + 1 < n_chunks else None
        rows = slice(c * CHUNK, (c + 1) * CHUNK)
        out = jnp.dot(cat, wout_ref[...], preferred_element_type=F32)
        o_ref[0, rows, :] = x_ref[0, rows, :] + mod_ref[5:6, :] * out
        cat = cat_next


def _mixer_body(x, mod_all, pz, q, k, v, g, states, dmask, wq, poolw, pscale, gng, dw, db, lng, lnb,
                w_out, *, layer, mod_row, tb=MIX_ROWS):
    bsz, seq, _ = x.shape
    tb = min(tb, seq)
    n_halo = seq // HALO
    hb = tb // HALO
    tok = lambda w: pl.BlockSpec((1, tb, w), lambda b, t: (b, t, 0))
    in_specs = [
        tok(D_MODEL),
        _mod_spec(layer, mod_row),
        tok(PZ_WIDTH),
        pl.BlockSpec((1, HALO, PZ_WIDTH), lambda b, t: (b, jnp.maximum(t * hb - 1, 0), 0)),
        pl.BlockSpec((1, HALO, PZ_WIDTH), lambda b, t: (b, jnp.minimum((t + 1) * hb, n_halo - 1), 0)),
        tok(RET_WIDTH), tok(RET_WIDTH), tok(RET_WIDTH), tok(RET_WIDTH),
        pl.BlockSpec((1, tb // CHUNK, RET_HEADS, HEAD_DIM, 2 * HEAD_DIM), lambda b, t: (b, t, 0, 0, 0)),
        _const_spec((RET_HEADS, CHUNK, CHUNK)),
        _const_spec((RET_HEADS, CHUNK, 2 * HEAD_DIM)),
        _const_spec((POOL_WIDTH, POOL_WIDTH)),
        _const_spec((1, POOL_WIDTH)),
        _const_spec((1, RET_WIDTH)),
        _const_spec((CONV_K + 1, CONV_WIDTH)),
        _const_spec((1, CONV_WIDTH)),
        _const_spec((1, CONV_WIDTH)),
        _const_spec((1, CONV_WIDTH)),
        _const_spec((D_MODEL, D_MODEL), (layer,)),
    ]
    return pl.pallas_call(
        functools.partial(_mix_kernel, seq_len=seq, tb=tb),
        grid=(bsz, seq // tb),
        in_specs=in_specs,
        out_specs=tok(D_MODEL),
        out_shape=jax.ShapeDtypeStruct(x.shape, F32),
        scratch_shapes=[pltpu.VMEM((tb + 2 * HALO, PZ_WIDTH), F32),
                        pltpu.VMEM((SUBLANES - 1, tb + 2 * HALO, CONV_WIDTH), F32),
                        pltpu.VMEM((tb, POOL_WIDTH), F32)],
        compiler_params=_cparams(2),
        name="mixer_body",
    )(x, mod_all, pz, pz, pz, q, k, v, g, states, dmask, wq, poolw, pscale, gng, dw, db, lng, lnb,
      w_out)


def _rope_tables(seq):
    n_freq = HEAD_DIM // 4
    inv = ROPE_BASE ** (-jnp.arange(n_freq, dtype=F32) / n_freq)
    rows = seq // GRID_W
    row = jnp.repeat(jnp.arange(rows, dtype=F32), GRID_W)
    col = jnp.tile(jnp.arange(GRID_W, dtype=F32), rows)
    ang_r = row[:, None] * inv[None]
    ang_c = col[:, None] * inv[None]
    cos = jnp.concatenate([jnp.cos(ang_r), jnp.cos(ang_c)] * 2, axis=-1)
    sin = jnp.concatenate([-jnp.sin(ang_r), -jnp.sin(ang_c), jnp.sin(ang_r), jnp.sin(ang_c)], axis=-1)
    return cos, sin


def _permute_heads(w_in):
    def perm(w):
        w = w.reshape(w.shape[:-1] + (RET_HEADS, 2, 2, HEAD_DIM // 4))
        return jnp.swapaxes(w, -2, -3).reshape(w.shape[:-4] + (RET_WIDTH,))
    return jnp.concatenate([w_in[..., :Q_OFF], perm(w_in[..., Q_OFF:K_OFF]),
                            perm(w_in[..., K_OFF:V_OFF]), w_in[..., V_OFF:]], axis=-1)


def _retention_tables(dec_f, dec_b):
    lg_f = jax.nn.log_sigmoid(dec_f.astype(F32))
    lg_b = jax.nn.log_sigmoid(dec_b.astype(F32))
    pos = jnp.arange(CHUNK, dtype=F32)
    diff = pos[:, None] - pos[None, :]
    dmask = jnp.where(diff[None] >= 0,
                      jnp.exp(jnp.maximum(diff, 0.0)[None] * lg_f[:, None, None]),
                      jnp.exp(jnp.maximum(-diff, 0.0)[None] * lg_b[:, None, None]))
    ones = jnp.ones((1, 1, HEAD_DIM), F32)
    wq_f = jnp.exp((pos + 1.0)[None, :] * lg_f[:, None])[:, :, None] * ones
    wq_b = jnp.exp((CHUNK - pos)[None, :] * lg_b[:, None])[:, :, None] * ones
    wq = jnp.concatenate([wq_f, wq_b], axis=-1)
    wk_f = jnp.exp((CHUNK - 1 - pos)[None, :] * lg_f[:, None])[:, :, None] * ones
    wk_b = jnp.exp(pos[None, :] * lg_b[:, None])[:, :, None] * ones
    wk = jnp.concatenate([wk_f, wk_b], axis=-1)
    cd = jnp.stack([jnp.repeat(jnp.exp(CHUNK * lg_f), HEAD_DIM),
                    jnp.repeat(jnp.exp(CHUNK * lg_b), HEAD_DIM)])
    return dmask, wq, wk, cd


def _block_diag(pool_w):
    out = jnp.zeros((POOL_WIDTH, POOL_WIDTH), pool_w.dtype)
    for gi in range(len(POOL_WINDOWS)):
        out = out.at[gi * POOL_GROUP:(gi + 1) * POOL_GROUP, gi * POOL_GROUP:(gi + 1) * POOL_GROUP].set(pool_w[gi])
    return out


def kernel(x, c, ctx, c_ctx, w_mod, b_mod, norm_g, ffn_w1, ffn_w3, ffn_w2, w_in, w_out, pool_w,
           pool_scale, ret_decay_fwd, ret_decay_bwd, ret_gn_g, conv_dw, conv_b, conv_ln_g,
           conv_ln_b, final_g):
    depth = w_mod.shape[0]
    bsz, seq, _ = x.shape
    cvec = jnp.zeros((MOD_ROWS, D_MODEL), F32).at[:bsz].set(c).at[bsz].set(c_ctx)
    mod_all = _modulation(cvec, w_mod, b_mod).reshape(depth, MOD_ROWS, N_MOD, D_MODEL)
    latent_row = lambda b: b
    context_row = lambda b: bsz
    rope_tabs = _rope_tables(seq)
    w1 = ffn_w1.astype(BF16)
    w3 = ffn_w3.astype(BF16)
    w2 = ffn_w2.astype(BF16)
    win = _permute_heads(w_in).astype(BF16)
    wout = w_out.astype(BF16)
    y = ctx
    for l in range(depth):
        last = l == depth - 1
        x, y = _ffn(x, y, mod_all, norm_g, w1, w3, w2, layer=l, half=0)

        dmask, wq, wk, cd = _retention_tables(ret_decay_fwd[l], ret_decay_bwd[l])
        pzx, qx, kx, vx, gx = _inproj(x, mod_all, norm_g, win, rope_tabs, layer=l, mod_row=latent_row,
                                      flatten=False)
        if last:
            ky, vy = _inproj(y, mod_all, norm_g, win, None, layer=l, mod_row=context_row,
                             flatten=True, kv_only=True)
        else:
            pzy, qy, ky, vy, gy = _inproj(y, mod_all, norm_g, win, None, layer=l, mod_row=context_row,
                                          flatten=True)
        sx, sy = _state_scan(kx, vx, ky, vy, wk, cd)
        params = (dmask, wq, _block_diag(pool_w[l]).astype(BF16), pool_scale[l].reshape(1, -1),
                  ret_gn_g[l].reshape(1, -1),
                  jnp.concatenate([conv_dw[l], jnp.zeros((1, CONV_WIDTH), F32)], axis=0),
                  conv_b[l].reshape(1, -1), conv_ln_g[l].reshape(1, -1), conv_ln_b[l].reshape(1, -1),
                  wout)
        x = _mixer_body(x, mod_all, pzx, qx, kx, vx, gx, sx, *params, layer=l, mod_row=latent_row)
        if last:
            x, _ = _ffn(x, None, mod_all, norm_g, w1, w3, w2, layer=l, half=1, final_g=final_g)
        else:
            y = _mixer_body(y, mod_all, pzy, qy, ky, vy, gy, sy, *params, layer=l, mod_row=context_row)
            x, y = _ffn(x, y, mod_all, norm_g, w1, w3, w2, layer=l, half=1)
    return x
```

```python
import functools

import jax
import jax.numpy as jnp
from jax import lax
from jax.experimental import pallas as pl
from jax.experimental.pallas import tpu as pltpu

F32 = jnp.float32
BF16 = jnp.bfloat16

D_MODEL = 1024
N_MOD = 9
D_FF = 2816
POOL_WIDTH = 256
POOL_WINDOWS = (2, 4, 8, 16)
POOL_GROUP = 64
RET_WIDTH = 512
RET_HEADS = 4
HEAD_DIM = 128
CHUNK = 128
CONV_WIDTH = 256
CONV_K = 31
IN_WIDTH = 2816
Q_OFF = POOL_WIDTH
K_OFF = Q_OFF + RET_WIDTH
V_OFF = K_OFF + RET_WIDTH
G_OFF = V_OFF + RET_WIDTH
C_OFF = G_OFF + RET_WIDTH
GRID_W = 64
ROPE_BASE = 10000.0
EPS = 1e-6

SUBLANES = 8
LANES = 128
FFN_ROWS = 512
FFN_SUB_ROWS = 128
INPROJ_ROWS = 1024
INPROJ_SUB_ROWS = 128
MIX_ROWS = 512
SCAN_UNROLL = 4
HALO = 16
PZ_WIDTH = POOL_WIDTH + CONV_WIDTH
MOD_ROWS = 24
VMEM_LIMIT = 56 * 1024 * 1024


def _cparams(n_axes):
    return pltpu.CompilerParams(dimension_semantics=("arbitrary",) * n_axes,
                                vmem_limit_bytes=VMEM_LIMIT)


def _const_spec(shape, lead=()):
    idx = tuple(lead) + (0,) * len(shape)
    return pl.BlockSpec((None,) * len(lead) + tuple(shape), lambda *_: idx,
                        pipeline_mode=pl.Buffered(1))


def _sigmoid(x):
    return 0.5 * jnp.tanh(0.5 * x) + 0.5


def _silu(x):
    hx = 0.5 * x
    return hx * jnp.tanh(hx) + hx


def _rms_mod(x, g, shift, scale):
    y = x * lax.rsqrt(jnp.mean(x * x, axis=-1, keepdims=True) + EPS) * g
    return y * (1.0 + scale) + shift


def _mod_kernel(c_ref, w_ref, b_ref, o_ref):
    s = _silu(c_ref[...]).astype(BF16)
    o_ref[0] = jnp.dot(s, w_ref[0].astype(BF16), preferred_element_type=F32) + b_ref[0]


def _modulation(cvec, w_mod, b_mod):
    depth = w_mod.shape[0]
    tn = 1152
    return pl.pallas_call(
        _mod_kernel,
        grid=(depth, N_MOD * D_MODEL // tn),
        in_specs=[pl.BlockSpec((MOD_ROWS, D_MODEL), lambda l, j: (0, 0)),
                  pl.BlockSpec((1, D_MODEL, tn), lambda l, j: (l, 0, j)),
                  pl.BlockSpec((1, 1, tn), lambda l, j: (l, 0, j))],
        out_specs=pl.BlockSpec((1, MOD_ROWS, tn), lambda l, j: (l, 0, j)),
        out_shape=jax.ShapeDtypeStruct((depth, MOD_ROWS, N_MOD * D_MODEL), F32),
        compiler_params=_cparams(2),
        name="modulation",
    )(cvec, w_mod, b_mod.reshape(depth, 1, N_MOD * D_MODEL))


def _mod_spec(layer, mod_row):
    return pl.BlockSpec((None, None, N_MOD, D_MODEL), lambda b, *_: (layer, mod_row(b), 0, 0))


def _ffn_kernel(*refs, half, sub, n_x, has_ctx, final):
    refs = list(refs)
    x_ref = refs.pop(0)
    y_ref = refs.pop(0) if has_ctx else None
    mod_ref, g_ref, w1_ref, w3_ref, w2_ref = refs[:5]
    fg_ref = refs[5] if final else None
    ox_ref = refs[-2] if has_ctx else refs[-1]
    oy_ref = refs[-1] if has_ctx else None

    shift = mod_ref[6 * half:6 * half + 1, :]
    scale = mod_ref[6 * half + 1:6 * half + 2, :]
    gate = mod_ref[6 * half + 2:6 * half + 3, :]

    def run(src_ref, dst_ref, fg_ref):
        for r0 in range(0, src_ref.shape[1], sub):
            rows = slice(r0, r0 + sub)
            x = src_ref[0, rows, :]
            h = _rms_mod(x, g_ref[2 * half:2 * half + 1, :], shift, scale).astype(BF16)
            a = jnp.dot(h, w1_ref[...], preferred_element_type=F32)
            b = jnp.dot(h, w3_ref[...], preferred_element_type=F32)
            u = (_silu(a) * b).astype(BF16)
            y = x + 0.5 * gate * jnp.dot(u, w2_ref[...], preferred_element_type=F32)
            if fg_ref is not None:
                y = y * lax.rsqrt(jnp.mean(y * y, axis=-1, keepdims=True) + EPS) * fg_ref[...]
            dst_ref[0, rows, :] = y

    if has_ctx:
        i = pl.program_id(0)
        pl.when(i < n_x)(lambda: run(x_ref, ox_ref, fg_ref))
        pl.when(i >= n_x)(lambda: run(y_ref, oy_ref, None))
    else:
        run(x_ref, ox_ref, fg_ref)


def _ffn(x, y, mod_all, norm_g, w1, w3, w2, *, layer, half, final_g=None, tm=FFN_ROWS):
    bsz, seq, _ = x.shape
    n_t = seq // tm
    n_x = bsz * n_t
    has_ctx = y is not None
    x_idx = lambda i: jnp.minimum(i, n_x - 1)
    x_spec = pl.BlockSpec((1, tm, D_MODEL), lambda i: (x_idx(i) // n_t, x_idx(i) % n_t, 0))
    y_spec = pl.BlockSpec((1, tm, D_MODEL), lambda i: (0, jnp.maximum(i - n_x, 0), 0))
    mod_spec = pl.BlockSpec((None, None, N_MOD, D_MODEL),
                            lambda i: (layer, jnp.where(i < n_x, x_idx(i) // n_t, bsz), 0, 0))
    in_specs, args = [x_spec], [x]
    out_specs, out_shape = [x_spec], [jax.ShapeDtypeStruct(x.shape, F32)]
    n_y = 0
    if has_ctx:
        y_flat = y.reshape(1, -1, D_MODEL)
        n_y = y_flat.shape[1] // tm
        in_specs.append(y_spec)
        args.append(y_flat)
        out_specs.append(y_spec)
        out_shape.append(jax.ShapeDtypeStruct(y_flat.shape, F32))
    in_specs += [mod_spec,
                 _const_spec((3, D_MODEL), (layer,)),
                 _const_spec((D_MODEL, D_FF), (layer, half)),
                 _const_spec((D_MODEL, D_FF), (layer, half)),
                 _const_spec((D_FF, D_MODEL), (layer, half))]
    args += [mod_all, norm_g, w1, w3, w2]
    if final_g is not None:
        in_specs.append(_const_spec((1, D_MODEL)))
        args.append(final_g.reshape(1, D_MODEL))
    outs = pl.pallas_call(
        functools.partial(_ffn_kernel, half=half, sub=FFN_SUB_ROWS, n_x=n_x, has_ctx=has_ctx,
                          final=final_g is not None),
        grid=(n_x + n_y,),
        in_specs=in_specs,
        out_specs=out_specs,
        out_shape=out_shape,
        compiler_params=_cparams(1),
        name="swiglu_half_step",
    )(*args)
    return (outs[0], outs[1].reshape(y.shape)) if has_ctx else (outs[0], None)


def _rope(t, cos, sin):
    return t * cos + pltpu.roll(t, HEAD_DIM // 2, 1) * sin


def _inproj_kernel(x_ref, mod_ref, g_ref, w_ref, *rest, rope, kv_only, sub):
    if kv_only:
        k_ref, v_ref = rest
    elif rope:
        cos_ref, sin_ref, pz_ref, q_ref, k_ref, v_ref, go_ref = rest
    else:
        pz_ref, q_ref, k_ref, v_ref, go_ref = rest
    k_scale = HEAD_DIM ** -0.5
    for r0 in range(0, x_ref.shape[1], sub):
        rows = slice(r0, r0 + sub)
        h = _rms_mod(x_ref[0, rows, :], g_ref[1:2, :], mod_ref[3:4, :], mod_ref[4:5, :]).astype(BF16)

        lo_all, hi_all = (K_OFF, G_OFF) if kv_only else (0, IN_WIDTH)
        p = jnp.dot(h, w_ref[:, lo_all:hi_all], preferred_element_type=F32)

        def proj(lo, hi):
            return p[:, lo - lo_all:hi - lo_all]

        heads = ((K_OFF, k_ref, k_scale),) if kv_only else ((Q_OFF, q_ref, None), (K_OFF, k_ref, k_scale))
        for off, dst, scale in heads:
            t = proj(off, off + RET_WIDTH)
            for hd in range(RET_HEADS):
                lanes = slice(hd * HEAD_DIM, (hd + 1) * HEAD_DIM)
                th = t[:, lanes]
                if rope:
                    th = _rope(th, cos_ref[rows, :], sin_ref[rows, :])
                if scale is not None:
                    th = th * scale
                dst[0, rows, lanes] = th.astype(BF16)
        v_ref[0, rows, :] = proj(V_OFF, G_OFF).astype(BF16)
        if kv_only:
            continue
        go_ref[0, rows, :] = proj(G_OFF, C_OFF).astype(BF16)
        pz_ref[0, rows, :POOL_WIDTH] = proj(0, POOL_WIDTH)
        glu = proj(C_OFF, IN_WIDTH)
        pz_ref[0, rows, POOL_WIDTH:] = glu[:, :CONV_WIDTH] * _sigmoid(glu[:, CONV_WIDTH:])


def _inproj(x, mod_all, norm_g, w_in, rope_tabs, *, layer, mod_row, flatten, kv_only=False,
            tm=INPROJ_ROWS):
    lead = x.shape[:2]
    if flatten:
        assert rope_tabs is None
        x = x.reshape(1, -1, D_MODEL)
    assert flatten or not kv_only
    bsz, seq, _ = x.shape
    in_specs = [pl.BlockSpec((1, tm, D_MODEL), lambda b, t: (b, t, 0)),
                _mod_spec(layer, mod_row),
                _const_spec((3, D_MODEL), (layer,)),
                _const_spec((D_MODEL, IN_WIDTH), (layer,))]
    args = [x, mod_all, norm_g, w_in]
    if rope_tabs is not None:
        in_specs += [pl.BlockSpec((tm, HEAD_DIM), lambda b, t: (t, 0))] * 2
        args += list(rope_tabs)
    tok = lambda w: pl.BlockSpec((1, tm, w), lambda b, t: (b, t, 0))
    n_ret = 2 if kv_only else 4
    out_specs = [tok(RET_WIDTH)] * n_ret
    out_shape = [jax.ShapeDtypeStruct((bsz, seq, RET_WIDTH), BF16)] * n_ret
    if not kv_only:
        out_specs = [tok(PZ_WIDTH)] + out_specs
        out_shape = [jax.ShapeDtypeStruct((bsz, seq, PZ_WIDTH), F32)] + out_shape
    outs = pl.pallas_call(
        functools.partial(_inproj_kernel, rope=rope_tabs is not None, kv_only=kv_only,
                          sub=INPROJ_SUB_ROWS),
        grid=(bsz, seq // tm),
        in_specs=in_specs,
        out_specs=out_specs,
        out_shape=out_shape,
        compiler_params=_cparams(2),
        name="mixer_in_proj",
    )(*args)
    return [o.reshape(lead + o.shape[2:]) for o in outs]


def _scan_kernel(kx_ref, vx_ref, ky_ref, vy_ref, wk_ref, cd_ref, sx_ref, sy_ref, kvx_ref, kvy_ref):
    n_x = kx_ref.shape[1] // CHUNK
    n_y = ky_ref.shape[1] // CHUNK

    def increments(k_ref, v_ref, kv_ref, n_chunks):
        unroll = min(SCAN_UNROLL, n_chunks)
        assert n_chunks % unroll == 0

        def body(i, carry):
            for u in range(unroll):
                n = i * unroll + u
                rows = pl.ds(pl.multiple_of(n * CHUNK, CHUNK), CHUNK)
                for hd in range(RET_HEADS):
                    lanes = slice(hd * HEAD_DIM, (hd + 1) * HEAD_DIM)
                    v = v_ref[0, rows, lanes].astype(F32)
                    vw = (jnp.concatenate([v, v], axis=-1) * wk_ref[hd]).astype(BF16)
                    kv_ref[n, hd] = lax.dot_general(k_ref[0, rows, lanes], vw, (((0,), (0,)), ((), ())),
                                                    preferred_element_type=F32)
            return carry
        lax.fori_loop(0, n_chunks // unroll, body, 0)

    increments(ky_ref, vy_ref, kvy_ref, n_y)
    increments(kx_ref, vx_ref, kvx_ref, n_x)

    for hd in range(RET_HEADS):
        for direction in (0, 1):
            lanes = slice(direction * HEAD_DIM, (direction + 1) * HEAD_DIM)
            decay = cd_ref[direction:direction + 1, hd * HEAD_DIM:(hd + 1) * HEAD_DIM]

            def scan(state, kv_ref, s_ref, n_chunks):
                def body(i, st):
                    n = i if direction == 0 else n_chunks - 1 - i
                    s_ref[0, n, hd, :, lanes] = st.astype(BF16)
                    return decay * st + kv_ref[n, hd, :, lanes]
                return lax.fori_loop(0, n_chunks, body, state)

            state = scan(jnp.zeros((HEAD_DIM, HEAD_DIM), F32), kvy_ref, sy_ref, n_y)
            scan(state, kvx_ref, sx_ref, n_x)


def _state_scan(kx, vx, ky, vy, wk, cd):
    bsz, seq, _ = kx.shape
    ctx = ky.shape[1]
    tok = lambda n: pl.BlockSpec((1, n, RET_WIDTH), lambda b: (b, 0, 0))
    st = lambda n: pl.BlockSpec((1, n, RET_HEADS, HEAD_DIM, 2 * HEAD_DIM), lambda b: (b, 0, 0, 0, 0))
    return pl.pallas_call(
        _scan_kernel,
        grid=(bsz,),
        in_specs=[tok(seq), tok(seq), tok(ctx), tok(ctx),
                  _const_spec((RET_HEADS, CHUNK, 2 * HEAD_DIM)), _const_spec((2, RET_WIDTH))],
        out_specs=[st(seq // CHUNK), st(ctx // CHUNK)],
        out_shape=[jax.ShapeDtypeStruct((bsz, seq // CHUNK, RET_HEADS, HEAD_DIM, 2 * HEAD_DIM), BF16),
                   jax.ShapeDtypeStruct((bsz, ctx // CHUNK, RET_HEADS, HEAD_DIM, 2 * HEAD_DIM), BF16)],
        scratch_shapes=[pltpu.VMEM((seq // CHUNK, RET_HEADS, HEAD_DIM, 2 * HEAD_DIM), F32),
                        pltpu.VMEM((ctx // CHUNK, RET_HEADS, HEAD_DIM, 2 * HEAD_DIM), F32)],
        compiler_params=_cparams(1),
        name="retention_state_scan",
    )(kx, vx, ky, vy, wk, cd)


def _mixer_rows(c, inv_ref, q_ref, k_ref, v_ref, g_ref, s_ref, dmask_ref, wq_ref,
                poolw_ref, pscale_ref, gng_ref, dw_ref, db_ref, lng_ref, lnb_ref, ext_ref,
                sh_ref, pool_ref):
    r0 = c * CHUNK
    rows = slice(r0, r0 + CHUNK)

    pooled = (pool_ref[rows, :] * inv_ref[rows, :]
              - ext_ref[HALO + r0:HALO + r0 + CHUNK, :POOL_WIDTH]).astype(BF16)
    mixed = jnp.dot(pooled, poolw_ref[...], preferred_element_type=F32) * pscale_ref[...]
    pieces = [mixed.astype(BF16)]

    for hd in range(RET_HEADS):
        lanes = slice(hd * HEAD_DIM, (hd + 1) * HEAD_DIM)
        qh = q_ref[0, rows, lanes]
        kh = k_ref[0, rows, lanes]
        vh = v_ref[0, rows, lanes]
        sc = lax.dot_general(qh, kh, (((1,), (1,)), ((), ())), preferred_element_type=F32)
        prob = (sc * dmask_ref[hd]).astype(BF16)
        o = jnp.dot(prob, vh, preferred_element_type=F32)
        cross = jnp.dot(qh, s_ref[0, c, hd], preferred_element_type=F32) * wq_ref[hd]
        o = o + cross[:, :HEAD_DIM] + cross[:, HEAD_DIM:]
        mu = jnp.mean(o, axis=-1, keepdims=True)
        cen = o - mu
        var = jnp.mean(cen * cen, axis=-1, keepdims=True)
        y = cen * lax.rsqrt(var + EPS) * gng_ref[:, lanes]
        y = y * _silu(g_ref[0, rows, lanes].astype(F32))
        pieces.append(y.astype(BF16))

    rb = 64
    for b0 in range(r0, r0 + CHUNK, rb):
        acc = jnp.zeros((rb, CONV_WIDTH), F32) + db_ref[...]
        for kk in range(CONV_K):
            lo = b0 + kk + HALO - CONV_K // 2
            r = lo % SUBLANES
            if r == 0:
                src = ext_ref[lo:lo + rb, POOL_WIDTH:]
            else:
                src = sh_ref[r - 1, lo - r:lo - r + rb, :]
            acc = acc + dw_ref[kk:kk + 1, :] * src
        mu = jnp.mean(acc, axis=-1, keepdims=True)
        cen = acc - mu
        var = jnp.mean(cen * cen, axis=-1, keepdims=True)
        zn = cen * lax.rsqrt(var + EPS) * lng_ref[...] + lnb_ref[...]
        pieces.append(_silu(zn).astype(BF16))
    conv = jnp.concatenate(pieces[1 + RET_HEADS:], axis=0)
    return jnp.concatenate(pieces[:1 + RET_HEADS] + [conv], axis=-1)


def _mix_kernel(x_ref, mod_ref, pz_ref, pzp_ref, pzn_ref, inv_ref, q_ref, k_ref, v_ref, g_ref, s_ref,
                dmask_ref, wq_ref, poolw_ref, pscale_ref, gng_ref, dw_ref, db_ref, lng_ref,
                lnb_ref, wout_ref, o_ref, ext_ref, sh_ref, pool_ref, *, tb):
    t = pl.program_id(1)
    n_t = pl.num_programs(1)

    ext_ref[0:HALO, :] = jnp.where(t > 0, pzp_ref[0], 0.0)
    ext_ref[HALO:HALO + tb, :] = pz_ref[0]
    ext_ref[HALO + tb:, :] = jnp.where(t < n_t - 1, pzn_ref[0], 0.0)
    n_sh = tb + 2 * HALO - SUBLANES
    for r in range(1, SUBLANES):
        sh_ref[r - 1, 0:n_sh, :] = ext_ref[r:r + n_sh, POOL_WIDTH:]

    n_ext = tb + 2 * HALO

    def pair(w, s):
        return pltpu.roll(w, s, 0) + pltpu.roll(w, n_ext - s, 0)

    low_half = lax.broadcasted_iota(jnp.int32, (n_ext, LANES), 1) < POOL_GROUP
    for col in range(POOL_WIDTH // LANES):
        lanes = slice(col * LANES, (col + 1) * LANES)
        e = ext_ref[:, lanes]
        w_lo = e + pltpu.roll(e, 1, 0)
        w_hi = pair(w_lo, 1)
        for _ in range(col):
            w_lo = pair(w_hi, 2)
            w_hi = pair(w_lo, 4)
        pool_ref[:, lanes] = jnp.where(low_half, w_lo, w_hi)[HALO:HALO + tb]

    def heads(c):
        return _mixer_rows(c, inv_ref, q_ref, k_ref, v_ref, g_ref, s_ref, dmask_ref, wq_ref,
                           poolw_ref, pscale_ref, gng_ref, dw_ref, db_ref, lng_ref, lnb_ref, ext_ref,
                           sh_ref, pool_ref)

    n_chunks = tb // CHUNK
    cat = heads(0)
    for c in range(n_chunks):
        cat_next = heads(c + 1) if c + 1 < n_chunks else None
        rows = slice(c * CHUNK, (c + 1) * CHUNK)
        out = jnp.dot(cat, wout_ref[...], preferred_element_type=F32)
        o_ref[0, rows, :] = x_ref[0, rows, :] + mod_ref[5:6, :] * out
        cat = cat_next


def _mixer_body(x, mod_all, pz, q, k, v, g, states, dmask, wq, poolw, pscale, gng, dw, db, lng, lnb,
                w_out, *, layer, mod_row, tb=MIX_ROWS):
    bsz, seq, _ = x.shape
    tb = min(tb, seq)
    n_halo = seq // HALO
    hb = tb // HALO
    tok = lambda w: pl.BlockSpec((1, tb, w), lambda b, t: (b, t, 0))
    in_specs = [
        tok(D_MODEL),
        _mod_spec(layer, mod_row),
        tok(PZ_WIDTH),
        pl.BlockSpec((1, HALO, PZ_WIDTH), lambda b, t: (b, jnp.maximum(t * hb - 1, 0), 0)),
        pl.BlockSpec((1, HALO, PZ_WIDTH), lambda b, t: (b, jnp.minimum((t + 1) * hb, n_halo - 1), 0)),
        pl.BlockSpec((tb, POOL_WIDTH), lambda b, t: (t, 0)),
        tok(RET_WIDTH), tok(RET_WIDTH), tok(RET_WIDTH), tok(RET_WIDTH),
        pl.BlockSpec((1, tb // CHUNK, RET_HEADS, HEAD_DIM, 2 * HEAD_DIM), lambda b, t: (b, t, 0, 0, 0)),
        _const_spec((RET_HEADS, CHUNK, CHUNK)),
        _const_spec((RET_HEADS, CHUNK, 2 * HEAD_DIM)),
        _const_spec((POOL_WIDTH, POOL_WIDTH)),
        _const_spec((1, POOL_WIDTH)),
        _const_spec((1, RET_WIDTH)),
        _const_spec((CONV_K + 1, CONV_WIDTH)),
        _const_spec((1, CONV_WIDTH)),
        _const_spec((1, CONV_WIDTH)),
        _const_spec((1, CONV_WIDTH)),
        _const_spec((D_MODEL, D_MODEL), (layer,)),
    ]
    return pl.pallas_call(
        functools.partial(_mix_kernel, tb=tb),
        grid=(bsz, seq // tb),
        in_specs=in_specs,
        out_specs=tok(D_MODEL),
        out_shape=jax.ShapeDtypeStruct(x.shape, F32),
        scratch_shapes=[pltpu.VMEM((tb + 2 * HALO, PZ_WIDTH), F32),
                        pltpu.VMEM((SUBLANES - 1, tb + 2 * HALO, CONV_WIDTH), F32),
                        pltpu.VMEM((tb, POOL_WIDTH), F32)],
        compiler_params=_cparams(2),
        name="mixer_body",
    )(x, mod_all, pz, pz, pz, _pool_inverse_counts(seq), q, k, v, g, states, dmask, wq, poolw, pscale,
      gng, dw, db, lng, lnb, w_out)


def _rope_tables(seq):
    n_freq = HEAD_DIM // 4
    inv = ROPE_BASE ** (-jnp.arange(n_freq, dtype=F32) / n_freq)
    rows = seq // GRID_W
    row = jnp.repeat(jnp.arange(rows, dtype=F32), GRID_W)
    col = jnp.tile(jnp.arange(GRID_W, dtype=F32), rows)
    ang_r = row[:, None] * inv[None]
    ang_c = col[:, None] * inv[None]
    cos = jnp.concatenate([jnp.cos(ang_r), jnp.cos(ang_c)] * 2, axis=-1)
    sin = jnp.concatenate([-jnp.sin(ang_r), -jnp.sin(ang_c), jnp.sin(ang_r), jnp.sin(ang_c)], axis=-1)
    return cos, sin


def _pool_inverse_counts(seq):
    t = jnp.arange(seq)[:, None]
    half = jnp.repeat(jnp.array([w // 2 for w in POOL_WINDOWS]), POOL_GROUP)[None, :]
    cnt = jnp.minimum(t + half, seq) - jnp.maximum(t - half, 0)
    return 1.0 / cnt.astype(F32)


def _permute_heads(w_in):
    def perm(w):
        w = w.reshape(w.shape[:-1] + (RET_HEADS, 2, 2, HEAD_DIM // 4))
        return jnp.swapaxes(w, -2, -3).reshape(w.shape[:-4] + (RET_WIDTH,))
    return jnp.concatenate([w_in[..., :Q_OFF], perm(w_in[..., Q_OFF:K_OFF]),
                            perm(w_in[..., K_OFF:V_OFF]), w_in[..., V_OFF:]], axis=-1)


def _retention_tables(dec_f, dec_b):
    lg_f = jax.nn.log_sigmoid(dec_f.astype(F32))
    lg_b = jax.nn.log_sigmoid(dec_b.astype(F32))
    pos = jnp.arange(CHUNK, dtype=F32)
    diff = pos[:, None] - pos[None, :]
    dmask = jnp.where(diff[None] >= 0,
                      jnp.exp(jnp.maximum(diff, 0.0)[None] * lg_f[:, None, None]),
                      jnp.exp(jnp.maximum(-diff, 0.0)[None] * lg_b[:, None, None]))
    ones = jnp.ones((1, 1, HEAD_DIM), F32)
    wq_f = jnp.exp((pos + 1.0)[None, :] * lg_f[:, None])[:, :, None] * ones
    wq_b = jnp.exp((CHUNK - pos)[None, :] * lg_b[:, None])[:, :, None] * ones
    wq = jnp.concatenate([wq_f, wq_b], axis=-1)
    wk_f = jnp.exp((CHUNK - 1 - pos)[None, :] * lg_f[:, None])[:, :, None] * ones
    wk_b = jnp.exp(pos[None, :] * lg_b[:, None])[:, :, None] * ones
    wk = jnp.concatenate([wk_f, wk_b], axis=-1)
    cd = jnp.stack([jnp.repeat(jnp.exp(CHUNK * lg_f), HEAD_DIM),
                    jnp.repeat(jnp.exp(CHUNK * lg_b), HEAD_DIM)])
    return dmask, wq, wk, cd


def _block_diag(pool_w):
    out = jnp.zeros((POOL_WIDTH, POOL_WIDTH), pool_w.dtype)
    for gi in range(len(POOL_WINDOWS)):
        out = out.at[gi * POOL_GROUP:(gi + 1) * POOL_GROUP, gi * POOL_GROUP:(gi + 1) * POOL_GROUP].set(pool_w[gi])
    return out


def kernel(x, c, ctx, c_ctx, w_mod, b_mod, norm_g, ffn_w1, ffn_w3, ffn_w2, w_in, w_out, pool_w,
           pool_scale, ret_decay_fwd, ret_decay_bwd, ret_gn_g, conv_dw, conv_b, conv_ln_g,
           conv_ln_b, final_g):
    depth = w_mod.shape[0]
    bsz, seq, _ = x.shape
    cvec = jnp.zeros((MOD_ROWS, D_MODEL), F32).at[:bsz].set(c).at[bsz].set(c_ctx)
    mod_all = _modulation(cvec, w_mod, b_mod).reshape(depth, MOD_ROWS, N_MOD, D_MODEL)
    latent_row = lambda b: b
    context_row = lambda b: bsz
    rope_tabs = _rope_tables(seq)
    w1 = ffn_w1.astype(BF16)
    w3 = ffn_w3.astype(BF16)
    w2 = ffn_w2.astype(BF16)
    win = _permute_heads(w_in).astype(BF16)
    wout = w_out.astype(BF16)
    y = ctx
    for l in range(depth):
        last = l == depth - 1
        x, y = _ffn(x, y, mod_all, norm_g, w1, w3, w2, layer=l, half=0)

        dmask, wq, wk, cd = _retention_tables(ret_decay_fwd[l], ret_decay_bwd[l])
        pzx, qx, kx, vx, gx = _inproj(x, mod_all, norm_g, win, rope_tabs, layer=l, mod_row=latent_row,
                                      flatten=False)
        if last:
            ky, vy = _inproj(y, mod_all, norm_g, win, None, layer=l, mod_row=context_row,
                             flatten=True, kv_only=True)
        else:
            pzy, qy, ky, vy, gy = _inproj(y, mod_all, norm_g, win, None, layer=l, mod_row=context_row,
                                          flatten=True)
        sx, sy = _state_scan(kx, vx, ky, vy, wk, cd)
        params = (dmask, wq, _block_diag(pool_w[l]).astype(BF16), pool_scale[l].reshape(1, -1),
                  ret_gn_g[l].reshape(1, -1),
                  jnp.concatenate([conv_dw[l], jnp.zeros((1, CONV_WIDTH), F32)], axis=0),
                  conv_b[l].reshape(1, -1), conv_ln_g[l].reshape(1, -1), conv_ln_b[l].reshape(1, -1),
                  wout)
        x = _mixer_body(x, mod_all, pzx, qx, kx, vx, gx, sx, *params, layer=l, mod_row=latent_row)
        if last:
            x, _ = _ffn(x, None, mod_all, norm_g, w1, w3, w2, layer=l, half=1, final_g=final_g)
        else:
            y = _mixer_body(y, mod_all, pzy, qy, ky, vy, gy, sy, *params, layer=l, mod_row=context_row)
            x, y = _ffn(x, y, mod_all, norm_g, w1, w3, w2, layer=l, half=1)
    return x
```

```python
import functools

import jax
import jax.numpy as jnp
from jax import lax
from jax.experimental import pallas as pl
from jax.experimental.pallas import tpu as pltpu

F32 = jnp.float32
BF16 = jnp.bfloat16

D_MODEL = 1024
N_MOD = 9
D_FF = 2816
POOL_WIDTH = 256
POOL_WINDOWS = (2, 4, 8, 16)
POOL_GROUP = 64
RET_WIDTH = 512
RET_HEADS = 4
HEAD_DIM = 128
CHUNK = 128
CONV_WIDTH = 256
CONV_K = 31
IN_WIDTH = 2816
Q_OFF = POOL_WIDTH
K_OFF = Q_OFF + RET_WIDTH
V_OFF = K_OFF + RET_WIDTH
G_OFF = V_OFF + RET_WIDTH
C_OFF = G_OFF + RET_WIDTH
GRID_W = 64
ROPE_BASE = 10000.0
EPS = 1e-6

SUBLANES = 8
LANES = 128
FFN_ROWS = 512
FFN_SUB_ROWS = 128
INPROJ_ROWS = 1024
INPROJ_SUB_ROWS = 128
MIX_ROWS = 1024
SCAN_UNROLL = 4
HALO = 16
PZ_WIDTH = POOL_WIDTH + CONV_WIDTH
MOD_ROWS = 24
VMEM_LIMIT = 56 * 1024 * 1024


def _cparams(n_axes):
    return pltpu.CompilerParams(dimension_semantics=("arbitrary",) * n_axes,
                                vmem_limit_bytes=VMEM_LIMIT)


def _const_spec(shape, lead=()):
    idx = tuple(lead) + (0,) * len(shape)
    return pl.BlockSpec((None,) * len(lead) + tuple(shape), lambda *_: idx,
                        pipeline_mode=pl.Buffered(1))


def _sigmoid(x):
    return 0.5 * jnp.tanh(0.5 * x) + 0.5


def _silu(x):
    hx = 0.5 * x
    return hx * jnp.tanh(hx) + hx


def _rms_mod(x, g, shift, scale):
    y = x * lax.rsqrt(jnp.mean(x * x, axis=-1, keepdims=True) + EPS) * g
    return y * (1.0 + scale) + shift


def _mod_kernel(c_ref, w_ref, b_ref, o_ref):
    s = _silu(c_ref[...]).astype(BF16)
    o_ref[0] = jnp.dot(s, w_ref[0].astype(BF16), preferred_element_type=F32) + b_ref[0]


def _modulation(cvec, w_mod, b_mod):
    depth = w_mod.shape[0]
    tn = 1152
    return pl.pallas_call(
        _mod_kernel,
        grid=(depth, N_MOD * D_MODEL // tn),
        in_specs=[pl.BlockSpec((MOD_ROWS, D_MODEL), lambda l, j: (0, 0)),
                  pl.BlockSpec((1, D_MODEL, tn), lambda l, j: (l, 0, j)),
                  pl.BlockSpec((1, 1, tn), lambda l, j: (l, 0, j))],
        out_specs=pl.BlockSpec((1, MOD_ROWS, tn), lambda l, j: (l, 0, j)),
        out_shape=jax.ShapeDtypeStruct((depth, MOD_ROWS, N_MOD * D_MODEL), F32),
        compiler_params=_cparams(2),
        name="modulation",
    )(cvec, w_mod, b_mod.reshape(depth, 1, N_MOD * D_MODEL))


def _mod_spec(layer, mod_row):
    return pl.BlockSpec((None, None, N_MOD, D_MODEL), lambda b, *_: (layer, mod_row(b), 0, 0))


def _ffn_kernel(*refs, half, sub, n_x, has_ctx, final):
    refs = list(refs)
    x_ref = refs.pop(0)
    y_ref = refs.pop(0) if has_ctx else None
    mod_ref, g_ref, w1_ref, w3_ref, w2_ref = refs[:5]
    fg_ref = refs[5] if final else None
    ox_ref = refs[-2] if has_ctx else refs[-1]
    oy_ref = refs[-1] if has_ctx else None

    shift = mod_ref[6 * half:6 * half + 1, :]
    scale = mod_ref[6 * half + 1:6 * half + 2, :]
    gate = mod_ref[6 * half + 2:6 * half + 3, :]

    def run(src_ref, dst_ref, fg_ref):
        for r0 in range(0, src_ref.shape[1], sub):
            rows = slice(r0, r0 + sub)
            x = src_ref[0, rows, :]
            h = _rms_mod(x, g_ref[2 * half:2 * half + 1, :], shift, scale).astype(BF16)
            a = jnp.dot(h, w1_ref[...], preferred_element_type=F32)
            b = jnp.dot(h, w3_ref[...], preferred_element_type=F32)
            u = (_silu(a) * b).astype(BF16)
            y = x + 0.5 * gate * jnp.dot(u, w2_ref[...], preferred_element_type=F32)
            if fg_ref is not None:
                y = y * lax.rsqrt(jnp.mean(y * y, axis=-1, keepdims=True) + EPS) * fg_ref[...]
            dst_ref[0, rows, :] = y

    if has_ctx:
        i = pl.program_id(0)
        pl.when(i < n_x)(lambda: run(x_ref, ox_ref, fg_ref))
        pl.when(i >= n_x)(lambda: run(y_ref, oy_ref, None))
    else:
        run(x_ref, ox_ref, fg_ref)


def _ffn(x, y, mod_all, norm_g, w1, w3, w2, *, layer, half, final_g=None, tm=FFN_ROWS):
    bsz, seq, _ = x.shape
    n_t = seq // tm
    n_x = bsz * n_t
    has_ctx = y is not None
    x_idx = lambda i: jnp.minimum(i, n_x - 1)
    x_spec = pl.BlockSpec((1, tm, D_MODEL), lambda i: (x_idx(i) // n_t, x_idx(i) % n_t, 0))
    y_spec = pl.BlockSpec((1, tm, D_MODEL), lambda i: (0, jnp.maximum(i - n_x, 0), 0))
    mod_spec = pl.BlockSpec((None, None, N_MOD, D_MODEL),
                            lambda i: (layer, jnp.where(i < n_x, x_idx(i) // n_t, bsz), 0, 0))
    in_specs, args = [x_spec], [x]
    out_specs, out_shape = [x_spec], [jax.ShapeDtypeStruct(x.shape, F32)]
    n_y = 0
    if has_ctx:
        y_flat = y.reshape(1, -1, D_MODEL)
        n_y = y_flat.shape[1] // tm
        in_specs.append(y_spec)
        args.append(y_flat)
        out_specs.append(y_spec)
        out_shape.append(jax.ShapeDtypeStruct(y_flat.shape, F32))
    in_specs += [mod_spec,
                 _const_spec((3, D_MODEL), (layer,)),
                 _const_spec((D_MODEL, D_FF), (layer, half)),
                 _const_spec((D_MODEL, D_FF), (layer, half)),
                 _const_spec((D_FF, D_MODEL), (layer, half))]
    args += [mod_all, norm_g, w1, w3, w2]
    if final_g is not None:
        in_specs.append(_const_spec((1, D_MODEL)))
        args.append(final_g.reshape(1, D_MODEL))
    outs = pl.pallas_call(
        functools.partial(_ffn_kernel, half=half, sub=FFN_SUB_ROWS, n_x=n_x, has_ctx=has_ctx,
                          final=final_g is not None),
        grid=(n_x + n_y,),
        in_specs=in_specs,
        out_specs=out_specs,
        out_shape=out_shape,
        compiler_params=_cparams(1),
        name="swiglu_half_step",
    )(*args)
    return (outs[0], outs[1].reshape(y.shape)) if has_ctx else (outs[0], None)


def _rope(t, cos, sin):
    return t * cos + pltpu.roll(t, HEAD_DIM // 2, 1) * sin


def _inproj_kernel(x_ref, mod_ref, g_ref, w_ref, *rest, rope, kv_only, sub):
    if kv_only:
        k_ref, v_ref = rest
    elif rope:
        cos_ref, sin_ref, pz_ref, q_ref, k_ref, v_ref, go_ref = rest
    else:
        pz_ref, q_ref, k_ref, v_ref, go_ref = rest
    k_scale = HEAD_DIM ** -0.5
    for r0 in range(0, x_ref.shape[1], sub):
        rows = slice(r0, r0 + sub)
        h = _rms_mod(x_ref[0, rows, :], g_ref[1:2, :], mod_ref[3:4, :], mod_ref[4:5, :]).astype(BF16)

        lo_all, hi_all = (K_OFF, G_OFF) if kv_only else (0, IN_WIDTH)
        p = jnp.dot(h, w_ref[:, lo_all:hi_all], preferred_element_type=F32)

        def proj(lo, hi):
            return p[:, lo - lo_all:hi - lo_all]

        heads = ((K_OFF, k_ref, k_scale),) if kv_only else ((Q_OFF, q_ref, None), (K_OFF, k_ref, k_scale))
        for off, dst, scale in heads:
            t = proj(off, off + RET_WIDTH)
            for hd in range(RET_HEADS):
                lanes = slice(hd * HEAD_DIM, (hd + 1) * HEAD_DIM)
                th = t[:, lanes]
                if rope:
                    th = _rope(th, cos_ref[rows, :], sin_ref[rows, :])
                if scale is not None:
                    th = th * scale
                dst[0, rows, lanes] = th.astype(BF16)
        v_ref[0, rows, :] = proj(V_OFF, G_OFF).astype(BF16)
        if kv_only:
            continue
        go_ref[0, rows, :] = proj(G_OFF, C_OFF).astype(BF16)
        pz_ref[0, rows, :POOL_WIDTH] = proj(0, POOL_WIDTH)
        glu = proj(C_OFF, IN_WIDTH)
        pz_ref[0, rows, POOL_WIDTH:] = glu[:, :CONV_WIDTH] * _sigmoid(glu[:, CONV_WIDTH:])


def _inproj(x, mod_all, norm_g, w_in, rope_tabs, *, layer, mod_row, flatten, kv_only=False,
            tm=INPROJ_ROWS):
    lead = x.shape[:2]
    if flatten:
        assert rope_tabs is None
        x = x.reshape(1, -1, D_MODEL)
    assert flatten or not kv_only
    bsz, seq, _ = x.shape
    in_specs = [pl.BlockSpec((1, tm, D_MODEL), lambda b, t: (b, t, 0)),
                _mod_spec(layer, mod_row),
                _const_spec((3, D_MODEL), (layer,)),
                _const_spec((D_MODEL, IN_WIDTH), (layer,))]
    args = [x, mod_all, norm_g, w_in]
    if rope_tabs is not None:
        in_specs += [pl.BlockSpec((tm, HEAD_DIM), lambda b, t: (t, 0))] * 2
        args += list(rope_tabs)
    tok = lambda w: pl.BlockSpec((1, tm, w), lambda b, t: (b, t, 0))
    n_ret = 2 if kv_only else 4
    out_specs = [tok(RET_WIDTH)] * n_ret
    out_shape = [jax.ShapeDtypeStruct((bsz, seq, RET_WIDTH), BF16)] * n_ret
    if not kv_only:
        out_specs = [tok(PZ_WIDTH)] + out_specs
        out_shape = [jax.ShapeDtypeStruct((bsz, seq, PZ_WIDTH), F32)] + out_shape
    outs = pl.pallas_call(
        functools.partial(_inproj_kernel, rope=rope_tabs is not None, kv_only=kv_only,
                          sub=INPROJ_SUB_ROWS),
        grid=(bsz, seq // tm),
        in_specs=in_specs,
        out_specs=out_specs,
        out_shape=out_shape,
        compiler_params=_cparams(2),
        name="mixer_in_proj",
    )(*args)
    return [o.reshape(lead + o.shape[2:]) for o in outs]


def _scan_kernel(kx_ref, vx_ref, ky_ref, vy_ref, wk_ref, cd_ref, sx_ref, sy_ref, kvx_ref, kvy_ref):
    n_x = kx_ref.shape[1] // CHUNK
    n_y = ky_ref.shape[1] // CHUNK

    def increments(k_ref, v_ref, kv_ref, n_chunks):
        unroll = min(SCAN_UNROLL, n_chunks)
        assert n_chunks % unroll == 0

        def body(i, carry):
            for u in range(unroll):
                n = i * unroll + u
                rows = pl.ds(pl.multiple_of(n * CHUNK, CHUNK), CHUNK)
                for hd in range(RET_HEADS):
                    lanes = slice(hd * HEAD_DIM, (hd + 1) * HEAD_DIM)
                    v = v_ref[0, rows, lanes].astype(F32)
                    vw = (jnp.concatenate([v, v], axis=-1) * wk_ref[hd]).astype(BF16)
                    kv_ref[n, hd] = lax.dot_general(k_ref[0, rows, lanes], vw, (((0,), (0,)), ((), ())),
                                                    preferred_element_type=F32)
            return carry
        lax.fori_loop(0, n_chunks // unroll, body, 0)

    increments(ky_ref, vy_ref, kvy_ref, n_y)
    increments(kx_ref, vx_ref, kvx_ref, n_x)

    for hd in range(RET_HEADS):
        for direction in (0, 1):
            lanes = slice(direction * HEAD_DIM, (direction + 1) * HEAD_DIM)
            decay = cd_ref[direction:direction + 1, hd * HEAD_DIM:(hd + 1) * HEAD_DIM]

            def scan(state, kv_ref, s_ref, n_chunks):
                def body(i, st):
                    n = i if direction == 0 else n_chunks - 1 - i
                    s_ref[0, n, hd, :, lanes] = st.astype(BF16)
                    return decay * st + kv_ref[n, hd, :, lanes]
                return lax.fori_loop(0, n_chunks, body, state)

            state = scan(jnp.zeros((HEAD_DIM, HEAD_DIM), F32), kvy_ref, sy_ref, n_y)
            scan(state, kvx_ref, sx_ref, n_x)


def _state_scan(kx, vx, ky, vy, wk, cd):
    bsz, seq, _ = kx.shape
    ctx = ky.shape[1]
    tok = lambda n: pl.BlockSpec((1, n, RET_WIDTH), lambda b: (b, 0, 0))
    st = lambda n: pl.BlockSpec((1, n, RET_HEADS, HEAD_DIM, 2 * HEAD_DIM), lambda b: (b, 0, 0, 0, 0))
    return pl.pallas_call(
        _scan_kernel,
        grid=(bsz,),
        in_specs=[tok(seq), tok(seq), tok(ctx), tok(ctx),
                  _const_spec((RET_HEADS, CHUNK, 2 * HEAD_DIM)), _const_spec((2, RET_WIDTH))],
        out_specs=[st(seq // CHUNK), st(ctx // CHUNK)],
        out_shape=[jax.ShapeDtypeStruct((bsz, seq // CHUNK, RET_HEADS, HEAD_DIM, 2 * HEAD_DIM), BF16),
                   jax.ShapeDtypeStruct((bsz, ctx // CHUNK, RET_HEADS, HEAD_DIM, 2 * HEAD_DIM), BF16)],
        scratch_shapes=[pltpu.VMEM((seq // CHUNK, RET_HEADS, HEAD_DIM, 2 * HEAD_DIM), F32),
                        pltpu.VMEM((ctx // CHUNK, RET_HEADS, HEAD_DIM, 2 * HEAD_DIM), F32)],
        compiler_params=_cparams(1),
        name="retention_state_scan",
    )(kx, vx, ky, vy, wk, cd)


def _mixer_rows(c, inv_ref, q_ref, k_ref, v_ref, g_ref, s_ref, dmask_ref, wq_ref,
                poolw_ref, pscale_ref, gng_ref, dw_ref, db_ref, lng_ref, lnb_ref, ext_ref,
                sh_ref, pool_ref):
    r0 = c * CHUNK
    rows = slice(r0, r0 + CHUNK)

    pooled = (pool_ref[rows, :] * inv_ref[rows, :]
              - ext_ref[HALO + r0:HALO + r0 + CHUNK, :POOL_WIDTH]).astype(BF16)
    mixed = jnp.dot(pooled, poolw_ref[...], preferred_element_type=F32) * pscale_ref[...]
    pieces = [mixed.astype(BF16)]

    for hd in range(RET_HEADS):
        lanes = slice(hd * HEAD_DIM, (hd + 1) * HEAD_DIM)
        qh = q_ref[0, rows, lanes]
        kh = k_ref[0, rows, lanes]
        vh = v_ref[0, rows, lanes]
        sc = lax.dot_general(qh, kh, (((1,), (1,)), ((), ())), preferred_element_type=F32)
        prob = (sc * dmask_ref[hd]).astype(BF16)
        o = jnp.dot(prob, vh, preferred_element_type=F32)
        cross = jnp.dot(qh, s_ref[0, c, hd], preferred_element_type=F32) * wq_ref[hd]
        o = o + cross[:, :HEAD_DIM] + cross[:, HEAD_DIM:]
        mu = jnp.mean(o, axis=-1, keepdims=True)
        cen = o - mu
        var = jnp.mean(cen * cen, axis=-1, keepdims=True)
        y = cen * lax.rsqrt(var + EPS) * gng_ref[:, lanes]
        y = y * _silu(g_ref[0, rows, lanes].astype(F32))
        pieces.append(y.astype(BF16))

    rb = 64
    for b0 in range(r0, r0 + CHUNK, rb):
        acc = jnp.zeros((rb, CONV_WIDTH), F32) + db_ref[...]
        for kk in range(CONV_K):
            lo = b0 + kk + HALO - CONV_K // 2
            r = lo % SUBLANES
            if r == 0:
                src = ext_ref[lo:lo + rb, POOL_WIDTH:]
            else:
                src = sh_ref[r - 1, lo - r:lo - r + rb, :]
            acc = acc + dw_ref[kk:kk + 1, :] * src
        mu = jnp.mean(acc, axis=-1, keepdims=True)
        cen = acc - mu
        var = jnp.mean(cen * cen, axis=-1, keepdims=True)
        zn = cen * lax.rsqrt(var + EPS) * lng_ref[...] + lnb_ref[...]
        pieces.append(_silu(zn).astype(BF16))
    conv = jnp.concatenate(pieces[1 + RET_HEADS:], axis=0)
    return jnp.concatenate(pieces[:1 + RET_HEADS] + [conv], axis=-1)


def _mix_kernel(x_ref, mod_ref, pz_ref, pzp_ref, pzn_ref, inv_ref, q_ref, k_ref, v_ref, g_ref, s_ref,
                dmask_ref, wq_ref, poolw_ref, pscale_ref, gng_ref, dw_ref, db_ref, lng_ref,
                lnb_ref, wout_ref, o_ref, ext_ref, sh_ref, pool_ref, *, tb):
    t = pl.program_id(1)
    n_t = pl.num_programs(1)

    ext_ref[0:HALO, :] = jnp.where(t > 0, pzp_ref[0], 0.0)
    ext_ref[HALO:HALO + tb, :] = pz_ref[0]
    ext_ref[HALO + tb:, :] = jnp.where(t < n_t - 1, pzn_ref[0], 0.0)
    n_sh = tb + 2 * HALO - SUBLANES
    for r in range(1, SUBLANES):
        sh_ref[r - 1, 0:n_sh, :] = ext_ref[r:r + n_sh, POOL_WIDTH:]

    n_ext = tb + 2 * HALO

    def pair(w, s):
        return pltpu.roll(w, s, 0) + pltpu.roll(w, n_ext - s, 0)

    low_half = lax.broadcasted_iota(jnp.int32, (n_ext, LANES), 1) < POOL_GROUP
    for col in range(POOL_WIDTH // LANES):
        lanes = slice(col * LANES, (col + 1) * LANES)
        e = ext_ref[:, lanes]
        w_lo = e + pltpu.roll(e, 1, 0)
        w_hi = pair(w_lo, 1)
        for _ in range(col):
            w_lo = pair(w_hi, 2)
            w_hi = pair(w_lo, 4)
        pool_ref[:, lanes] = jnp.where(low_half, w_lo, w_hi)[HALO:HALO + tb]

    def heads(c):
        return _mixer_rows(c, inv_ref, q_ref, k_ref, v_ref, g_ref, s_ref, dmask_ref, wq_ref,
                           poolw_ref, pscale_ref, gng_ref, dw_ref, db_ref, lng_ref, lnb_ref, ext_ref,
                           sh_ref, pool_ref)

    n_chunks = tb // CHUNK
    cat = heads(0)
    for c in range(n_chunks):
        cat_next = heads(c + 1) if c + 1 < n_chunks else None
        rows = slice(c * CHUNK, (c + 1) * CHUNK)
        out = jnp.dot(cat, wout_ref[...], preferred_element_type=F32)
        o_ref[0, rows, :] = x_ref[0, rows, :] + mod_ref[5:6, :] * out
        cat = cat_next


def _mixer_body(x, mod_all, pz, q, k, v, g, states, dmask, wq, poolw, pscale, gng, dw, db, lng, lnb,
                w_out, *, layer, mod_row, tb=MIX_ROWS):
    bsz, seq, _ = x.shape
    tb = min(tb, seq)
    n_halo = seq // HALO
    hb = tb // HALO
    tok = lambda w: pl.BlockSpec((1, tb, w), lambda b, t: (b, t, 0))
    in_specs = [
        tok(D_MODEL),
        _mod_spec(layer, mod_row),
        tok(PZ_WIDTH),
        pl.BlockSpec((1, HALO, PZ_WIDTH), lambda b, t: (b, jnp.maximum(t * hb - 1, 0), 0)),
        pl.BlockSpec((1, HALO, PZ_WIDTH), lambda b, t: (b, jnp.minimum((t + 1) * hb, n_halo - 1), 0)),
        pl.BlockSpec((tb, POOL_WIDTH), lambda b, t: (t, 0)),
        tok(RET_WIDTH), tok(RET_WIDTH), tok(RET_WIDTH), tok(RET_WIDTH),
        pl.BlockSpec((1, tb // CHUNK, RET_HEADS, HEAD_DIM, 2 * HEAD_DIM), lambda b, t: (b, t, 0, 0, 0)),
        _const_spec((RET_HEADS, CHUNK, CHUNK)),
        _const_spec((RET_HEADS, CHUNK, 2 * HEAD_DIM)),
        _const_spec((POOL_WIDTH, POOL_WIDTH)),
        _const_spec((1, POOL_WIDTH)),
        _const_spec((1, RET_WIDTH)),
        _const_spec((CONV_K + 1, CONV_WIDTH)),
        _const_spec((1, CONV_WIDTH)),
        _const_spec((1, CONV_WIDTH)),
        _const_spec((1, CONV_WIDTH)),
        _const_spec((D_MODEL, D_MODEL), (layer,)),
    ]
    return pl.pallas_call(
        functools.partial(_mix_kernel, tb=tb),
        grid=(bsz, seq // tb),
        in_specs=in_specs,
        out_specs=tok(D_MODEL),
        out_shape=jax.ShapeDtypeStruct(x.shape, F32),
        scratch_shapes=[pltpu.VMEM((tb + 2 * HALO, PZ_WIDTH), F32),
                        pltpu.VMEM((SUBLANES - 1, tb + 2 * HALO, CONV_WIDTH), F32),
                        pltpu.VMEM((tb, POOL_WIDTH), F32)],
        compiler_params=_cparams(2),
        name="mixer_body",
    )(x, mod_all, pz, pz, pz, _pool_inverse_counts(seq), q, k, v, g, states, dmask, wq, poolw, pscale,
      gng, dw, db, lng, lnb, w_out)


def _rope_tables(seq):
    n_freq = HEAD_DIM // 4
    inv = ROPE_BASE ** (-jnp.arange(n_freq, dtype=F32) / n_freq)
    rows = seq // GRID_W
    row = jnp.repeat(jnp.arange(rows, dtype=F32), GRID_W)
    col = jnp.tile(jnp.arange(GRID_W, dtype=F32), rows)
    ang_r = row[:, None] * inv[None]
    ang_c = col[:, None] * inv[None]
    cos = jnp.concatenate([jnp.cos(ang_r), jnp.cos(ang_c)] * 2, axis=-1)
    sin = jnp.concatenate([-jnp.sin(ang_r), -jnp.sin(ang_c), jnp.sin(ang_r), jnp.sin(ang_c)], axis=-1)
    return cos, sin


def _pool_inverse_counts(seq):
    t = jnp.arange(seq)[:, None]
    half = jnp.repeat(jnp.array([w // 2 for w in POOL_WINDOWS]), POOL_GROUP)[None, :]
    cnt = jnp.minimum(t + half, seq) - jnp.maximum(t - half, 0)
    return 1.0 / cnt.astype(F32)


def _permute_heads(w_in):
    def perm(w):
        w = w.reshape(w.shape[:-1] + (RET_HEADS, 2, 2, HEAD_DIM // 4))
        return jnp.swapaxes(w, -2, -3).reshape(w.shape[:-4] + (RET_WIDTH,))
    return jnp.concatenate([w_in[..., :Q_OFF], perm(w_in[..., Q_OFF:K_OFF]),
                            perm(w_in[..., K_OFF:V_OFF]), w_in[..., V_OFF:]], axis=-1)


def _retention_tables(dec_f, dec_b):
    lg_f = jax.nn.log_sigmoid(dec_f.astype(F32))
    lg_b = jax.nn.log_sigmoid(dec_b.astype(F32))
    pos = jnp.arange(CHUNK, dtype=F32)
    diff = pos[:, None] - pos[None, :]
    dmask = jnp.where(diff[None] >= 0,
                      jnp.exp(jnp.maximum(diff, 0.0)[None] * lg_f[:, None, None]),
                      jnp.exp(jnp.maximum(-diff, 0.0)[None] * lg_b[:, None, None]))
    ones = jnp.ones((1, 1, HEAD_DIM), F32)
    wq_f = jnp.exp((pos + 1.0)[None, :] * lg_f[:, None])[:, :, None] * ones
    wq_b = jnp.exp((CHUNK - pos)[None, :] * lg_b[:, None])[:, :, None] * ones
    wq = jnp.concatenate([wq_f, wq_b], axis=-1)
    wk_f = jnp.exp((CHUNK - 1 - pos)[None, :] * lg_f[:, None])[:, :, None] * ones
    wk_b = jnp.exp(pos[None, :] * lg_b[:, None])[:, :, None] * ones
    wk = jnp.concatenate([wk_f, wk_b], axis=-1)
    cd = jnp.stack([jnp.repeat(jnp.exp(CHUNK * lg_f), HEAD_DIM),
                    jnp.repeat(jnp.exp(CHUNK * lg_b), HEAD_DIM)])
    return dmask, wq, wk, cd


def _block_diag(pool_w):
    out = jnp.zeros((POOL_WIDTH, POOL_WIDTH), pool_w.dtype)
    for gi in range(len(POOL_WINDOWS)):
        out = out.at[gi * POOL_GROUP:(gi + 1) * POOL_GROUP, gi * POOL_GROUP:(gi + 1) * POOL_GROUP].set(pool_w[gi])
    return out


def kernel(x, c, ctx, c_ctx, w_mod, b_mod, norm_g, ffn_w1, ffn_w3, ffn_w2, w_in, w_out, pool_w,
           pool_scale, ret_decay_fwd, ret_decay_bwd, ret_gn_g, conv_dw, conv_b, conv_ln_g,
           conv_ln_b, final_g):
    depth = w_mod.shape[0]
    bsz, seq, _ = x.shape
    cvec = jnp.zeros((MOD_ROWS, D_MODEL), F32).at[:bsz].set(c).at[bsz].set(c_ctx)
    mod_all = _modulation(cvec, w_mod, b_mod).reshape(depth, MOD_ROWS, N_MOD, D_MODEL)
    latent_row = lambda b: b
    context_row = lambda b: bsz
    rope_tabs = _rope_tables(seq)
    w1 = ffn_w1.astype(BF16)
    w3 = ffn_w3.astype(BF16)
    w2 = ffn_w2.astype(BF16)
    win = _permute_heads(w_in).astype(BF16)
    wout = w_out.astype(BF16)
    y = ctx
    for l in range(depth):
        last = l == depth - 1
        x, y = _ffn(x, y, mod_all, norm_g, w1, w3, w2, layer=l, half=0)

        dmask, wq, wk, cd = _retention_tables(ret_decay_fwd[l], ret_decay_bwd[l])
        pzx, qx, kx, vx, gx = _inproj(x, mod_all, norm_g, win, rope_tabs, layer=l, mod_row=latent_row,
                                      flatten=False)
        if last:
            ky, vy = _inproj(y, mod_all, norm_g, win, None, layer=l, mod_row=context_row,
                             flatten=True, kv_only=True)
        else:
            pzy, qy, ky, vy, gy = _inproj(y, mod_all, norm_g, win, None, layer=l, mod_row=context_row,
                                          flatten=True)
        sx, sy = _state_scan(kx, vx, ky, vy, wk, cd)
        params = (dmask, wq, _block_diag(pool_w[l]).astype(BF16), pool_scale[l].reshape(1, -1),
                  ret_gn_g[l].reshape(1, -1),
                  jnp.concatenate([conv_dw[l], jnp.zeros((1, CONV_WIDTH), F32)], axis=0),
                  conv_b[l].reshape(1, -1), conv_ln_g[l].reshape(1, -1), conv_ln_b[l].reshape(1, -1),
                  wout)
        x = _mixer_body(x, mod_all, pzx, qx, kx, vx, gx, sx, *params, layer=l, mod_row=latent_row)
        if last:
            x, _ = _ffn(x, None, mod_all, norm_g, w1, w3, w2, layer=l, half=1, final_g=final_g)
        else:
            y = _mixer_body(y, mod_all, pzy, qy, ky, vy, gy, sy, *params, layer=l, mod_row=context_row)
            x, y = _ffn(x, y, mod_all, norm_g, w1, w3, w2, layer=l, half=1)
    return x
```
